```python
import math
import jax, jax.numpy as jnp
from jax import lax
import numpy as np

D_MODEL = 2048
BATCH = 4
SEQ = 2048
DEPTH = 1

CHUNK = 64
Q_BLOCK = 128
D_MIX = D_MODEL
SSM_WIDTH = D_MIX // 2
SSM_GROUP = 16
SSM_GROUPS = SSM_WIDTH // SSM_GROUP
SSM_STATE = 64
N_HEADS = 8
QK_NOPE = 128
QK_ROPE = 64
QK_HEAD = QK_NOPE + QK_ROPE
V_HEAD = 128
ATTN_WIDTH = N_HEADS * V_HEAD
Q_LORA = 512
KV_LORA = 256
D_IN = SSM_WIDTH + Q_LORA + KV_LORA + QK_ROPE
D_FF = 5632
ROPE_THETA = 10000.0
EPS = 1e-6
STEP_MIN = 1e-3
STEP_MAX = 1e-1

kernel_name = "hymba_s5_mla_macaron_block"


def rms_norm(x, g):
    xf = x.astype(jnp.float32)
    y = xf * lax.rsqrt(jnp.mean(xf * xf, axis=-1, keepdims=True) + EPS)
    return (y * g.astype(jnp.float32)).astype(x.dtype)


def swiglu(h, w_gate, w_up, w_down):
    return (jax.nn.silu(h @ w_gate) * (h @ w_up)) @ w_down


def rope(x, cos, sin):
    half = x.shape[-1] // 2
    x1, x2 = x[..., :half], x[..., half:]
    return jnp.concatenate([x1 * cos - x2 * sin, x2 * cos + x1 * sin], axis=-1)


def s5_mixer(u, log_step, a_re, a_im, b_re, b_im, c_re, c_im, d_skip, w_glu, b_glu):
    f32 = jnp.float32
    bsz, seq, _ = u.shape
    uf = u.astype(f32).reshape(bsz, seq, SSM_GROUPS, SSM_GROUP)
    dt = jnp.exp(log_step.astype(f32))[:, None]
    ar, ai = a_re.astype(f32), a_im.astype(f32)
    mag = jnp.exp(ar * dt)
    ang = ai * dt
    lr, li = mag * jnp.cos(ang), mag * jnp.sin(ang)
    den = ar * ar + ai * ai
    fr = ((lr - 1.0) * ar + li * ai) / den
    fi = (li * ar - (lr - 1.0) * ai) / den
    bu_r = jnp.einsum('blgc,gnc->blgn', uf, b_re.astype(f32))
    bu_i = jnp.einsum('blgc,gnc->blgn', uf, b_im.astype(f32))
    br = fr * bu_r - fi * bu_i
    bi = fr * bu_i + fi * bu_r
    lr_t = jnp.broadcast_to(lr, (1, seq, SSM_GROUPS, SSM_STATE))
    li_t = jnp.broadcast_to(li, (1, seq, SSM_GROUPS, SSM_STATE))

    def combine(e1, e2):
        a1r, a1i, b1r, b1i = e1
        a2r, a2i, b2r, b2i = e2
        return (a2r * a1r - a2i * a1i,
                a2r * a1i + a2i * a1r,
                a2r * b1r - a2i * b1i + b2r,
                a2r * b1i + a2i * b1r + b2i)

    _, _, sr, si = lax.associative_scan(combine, (lr_t, li_t, br, bi), axis=1)
    y = (jnp.einsum('blgn,gcn->blgc', sr, c_re.astype(f32))
         - jnp.einsum('blgn,gcn->blgc', si, c_im.astype(f32))
         + d_skip.astype(f32) * uf)
    y = jax.nn.gelu(y.reshape(bsz, seq, SSM_WIDTH))
    y = y * jax.nn.sigmoid(y @ w_glu.astype(f32) + b_glu.astype(f32))
    return y.astype(u.dtype)


def mla_mixer(q_lat, kv_lat, k_pe, cos, sin, q_a_norm, w_q_up, kv_a_norm, w_kv_up, q_norm, k_norm):
    bsz, seq, _ = q_lat.shape
    q = (rms_norm(q_lat, q_a_norm) @ w_q_up).reshape(bsz, seq, N_HEADS, QK_HEAD)
    kv = (rms_norm(kv_lat, kv_a_norm) @ w_kv_up).reshape(bsz, seq, N_HEADS, QK_NOPE + V_HEAD)
    k_nope, v = kv[..., :QK_NOPE], kv[..., QK_NOPE:]
    k = jnp.concatenate(
        [k_nope, jnp.broadcast_to(k_pe[:, :, None, :], (bsz, seq, N_HEADS, QK_ROPE))], axis=-1)
    q = rms_norm(q, q_norm)
    k = rms_norm(k, k_norm)
    cos_c, sin_c = cos.astype(q.dtype), sin.astype(q.dtype)
    q = jnp.concatenate([q[..., :QK_NOPE], rope(q[..., QK_NOPE:], cos_c, sin_c)], axis=-1)
    k = jnp.concatenate([k[..., :QK_NOPE], rope(k[..., QK_NOPE:], cos_c, sin_c)], axis=-1)
    q = q.transpose(0, 2, 1, 3)
    k = k.transpose(0, 2, 1, 3)
    v = v.transpose(0, 2, 1, 3)
    scale = QK_HEAD ** -0.5
    outs = []
    for i in range(seq // Q_BLOCK):
        q0 = i * Q_BLOCK
        k_end = q0 + Q_BLOCK
        s = jnp.einsum('bhqd,bhkd->bhqk', q[:, :, q0:k_end], k[:, :, :k_end]).astype(jnp.float32) * scale
        q_chunk = (q0 + jnp.arange(Q_BLOCK)) // CHUNK
        k_chunk = jnp.arange(k_end) // CHUNK
        s = jnp.where(k_chunk[None, :] <= q_chunk[:, None], s, -jnp.inf)
        p = jax.nn.softmax(s, axis=-1).astype(v.dtype)
        outs.append(jnp.einsum('bhqk,bhkd->bhqd', p, v[:, :, :k_end]))
    o = jnp.concatenate(outs, axis=2)
    return o.transpose(0, 2, 1, 3).reshape(bsz, seq, ATTN_WIDTH)


def setup_inputs(seed: int = 0) -> dict:
    key = jax.random.key(seed)
    ks = jax.random.split(key, 40)
    f32 = jnp.float32

    def dense(k, fan_in, fan_out):
        return jax.random.normal(k, (DEPTH, fan_in, fan_out), f32) * fan_in ** -0.5

    def gain(k, n):
        return 1.0 + 0.01 * jax.random.normal(k, (DEPTH, n), f32)

    G, N, C = SSM_GROUPS, SSM_STATE, SSM_GROUP
    x = jax.random.normal(ks[0], (BATCH, SEQ, D_MODEL), f32)
    offsets = jax.random.randint(ks[1], (BATCH, 1), 0, 4096, dtype=jnp.int32)
    positions = (offsets + jnp.arange(SEQ, dtype=jnp.int32)[None, :]).astype(jnp.int32)
    return {
        "x": x,
        "positions": positions,
        "ffn1_norm": gain(ks[2], D_MODEL),
        "ffn1_w_gate": dense(ks[3], D_MODEL, D_FF),
        "ffn1_w_up": dense(ks[4], D_MODEL, D_FF),
        "ffn1_w_down": dense(ks[5], D_FF, D_MODEL),
        "mix_norm": gain(ks[6], D_MODEL),
        "w_in": dense(ks[7], D_MODEL, D_IN),
        "ssm_log_step": jax.random.uniform(ks[8], (DEPTH, G), f32,
                                           minval=math.log(STEP_MIN), maxval=math.log(STEP_MAX)),
        "ssm_a_re": -0.5 + 0.01 * jax.random.normal(ks[9], (DEPTH, G, N), f32),
        "ssm_a_im": math.pi * jnp.arange(N, dtype=f32)[None, None, :]
                    + 0.01 * jax.random.normal(ks[10], (DEPTH, G, N), f32),
        "ssm_b_re": jax.random.normal(ks[11], (DEPTH, G, N, C), f32) * (2 * C) ** -0.5,
        "ssm_b_im": jax.random.normal(ks[12], (DEPTH, G, N, C), f32) * (2 * C) ** -0.5,
        "ssm_c_re": jax.random.normal(ks[13], (DEPTH, G, C, N), f32) * (2 * N) ** -0.5,
        "ssm_c_im": jax.random.normal(ks[14], (DEPTH, G, C, N), f32) * (2 * N) ** -0.5,
        "ssm_d": jax.random.normal(ks[15], (DEPTH, G, C), f32),
        "ssm_w_glu": dense(ks[16], SSM_WIDTH, SSM_WIDTH),
        "ssm_b_glu": 0.01 * jax.random.normal(ks[17], (DEPTH, SSM_WIDTH), f32),
        "mla_q_a_norm": gain(ks[18], Q_LORA),
        "mla_w_q_up": dense(ks[19], Q_LORA, N_HEADS * QK_HEAD),
        "mla_kv_a_norm": gain(ks[20], KV_LORA),
        "mla_w_kv_up": dense(ks[21], KV_LORA, N_HEADS * (QK_NOPE + V_HEAD)),
        "mla_q_norm": gain(ks[22], QK_HEAD),
        "mla_k_norm": gain(ks[23], QK_HEAD),
        "ssm_out_norm": gain(ks[24], SSM_WIDTH),
        "attn_out_norm": gain(ks[25], ATTN_WIDTH),
        "w_out": dense(ks[26], D_MIX, D_MODEL),
        "ffn2_norm": gain(ks[27], D_MODEL),
        "ffn2_w_gate": dense(ks[28], D_MODEL, D_FF),
        "ffn2_w_up": dense(ks[29], D_MODEL, D_FF),
        "ffn2_w_down": dense(ks[30], D_FF, D_MODEL),
        "final_norm": gain(ks[31], D_MODEL),
    }


def reference(x, positions, ffn1_norm, ffn1_w_gate, ffn1_w_up, ffn1_w_down, mix_norm, w_in,
              ssm_log_step, ssm_a_re, ssm_a_im, ssm_b_re, ssm_b_im, ssm_c_re, ssm_c_im, ssm_d,
              ssm_w_glu, ssm_b_glu, mla_q_a_norm, mla_w_q_up, mla_kv_a_norm, mla_w_kv_up,
              mla_q_norm, mla_k_norm, ssm_out_norm, attn_out_norm, w_out,
              ffn2_norm, ffn2_w_gate, ffn2_w_up, ffn2_w_down, final_norm):
    inv_freq = ROPE_THETA ** (-jnp.arange(0, QK_ROPE, 2, dtype=jnp.float32) / QK_ROPE)
    ang = positions.astype(jnp.float32)[..., None] * inv_freq
    cos = jnp.cos(ang)[:, :, None, :]
    sin = jnp.sin(ang)[:, :, None, :]
    o1 = SSM_WIDTH
    o2 = o1 + Q_LORA
    o3 = o2 + KV_LORA
    for l in range(DEPTH):
        x = x + 0.5 * swiglu(rms_norm(x, ffn1_norm[l]), ffn1_w_gate[l], ffn1_w_up[l], ffn1_w_down[l])
        z = rms_norm(x, mix_norm[l]) @ w_in[l]
        y_ssm = s5_mixer(z[..., :o1], ssm_log_step[l], ssm_a_re[l], ssm_a_im[l], ssm_b_re[l],
                         ssm_b_im[l], ssm_c_re[l], ssm_c_im[l], ssm_d[l], ssm_w_glu[l], ssm_b_glu[l])
        y_att = mla_mixer(z[..., o1:o2], z[..., o2:o3], z[..., o3:], cos, sin,
                          mla_q_a_norm[l], mla_w_q_up[l], mla_kv_a_norm[l], mla_w_kv_up[l],
                          mla_q_norm[l], mla_k_norm[l])
        y = jnp.concatenate([rms_norm(y_ssm, ssm_out_norm[l]), rms_norm(y_att, attn_out_norm[l])], axis=-1)
        x = x + y @ w_out[l]
        x = x + 0.5 * swiglu(rms_norm(x, ffn2_norm[l]), ffn2_w_gate[l], ffn2_w_up[l], ffn2_w_down[l])
        x = rms_norm(x, final_norm[l])
    return x
```

```python
import functools
import math

import jax
import jax.numpy as jnp
from jax import lax
from jax.experimental import pallas as pl
from jax.experimental.pallas import tpu as pltpu

D_MODEL = 2048
CHUNK = 64
SSM_WIDTH = 1024
SSM_GROUP = 16
SSM_GROUPS = 64
SSM_STATE = 64
N_HEADS = 8
QK_NOPE = 128
QK_ROPE = 64
QK_HEAD = QK_NOPE + QK_ROPE
V_HEAD = 128
ATTN_WIDTH = N_HEADS * V_HEAD
Q_LORA = 512
KV_LORA = 256
D_FF = 5632
ROPE_THETA = 10000.0
EPS = 1e-6

LANES = 128
SUBLANES = 8
MXU_DIM = 256
HEAD_PAD = 2 * LANES
VMEM_LIMIT = 56 * 1024 * 1024

BF16 = jnp.bfloat16
F32 = jnp.float32


def _rms(xf, g):
    return xf * lax.rsqrt(jnp.mean(xf * xf, axis=-1, keepdims=True) + EPS) * g


def _const_spec(shape):
    nd = len(shape)
    return pl.BlockSpec(shape, lambda *_: (0,) * nd, pipeline_mode=pl.Buffered(1))


FFN_TM = 1024
FFN_TF = 512


def _ffn_kernel(x_ref, g_ref, wg_ref, wu_ref, wd_ref, fg_ref, o_ref, h_ref, *, final_norm):
    j = pl.program_id(1)

    @pl.when(j == 0)
    def _():
        h_ref[...] = _rms(x_ref[...], g_ref[...]).astype(BF16)
        o_ref[...] = jnp.zeros_like(o_ref)

    h = h_ref[...]
    gate = jnp.dot(h, wg_ref[...], preferred_element_type=F32)
    up = jnp.dot(h, wu_ref[...], preferred_element_type=F32)
    act = (gate * jax.nn.sigmoid(gate) * up).astype(BF16)
    o_ref[...] += jnp.dot(act, wd_ref[...], preferred_element_type=F32)

    @pl.when(j == pl.num_programs(1) - 1)
    def _():
        y = x_ref[...] + 0.5 * o_ref[...]
        if final_norm:
            y = _rms(y, fg_ref[...])
        o_ref[...] = y


def _ffn(x, g, wg, wu, wd, fg, final_norm):
    t, d = x.shape
    dff = wg.shape[1]
    grid = (t // FFN_TM, dff // FFN_TF)
    return pl.pallas_call(
        functools.partial(_ffn_kernel, final_norm=final_norm),
        grid=grid,
        in_specs=[
            pl.BlockSpec((FFN_TM, d), lambda i, j: (i, 0), pipeline_mode=pl.Buffered(1)),
            pl.BlockSpec((1, d), lambda i, j: (0, 0)),
            pl.BlockSpec((d, FFN_TF), lambda i, j: (0, j)),
            pl.BlockSpec((d, FFN_TF), lambda i, j: (0, j)),
            pl.BlockSpec((FFN_TF, d), lambda i, j: (j, 0)),
            pl.BlockSpec((1, d), lambda i, j: (0, 0)),
        ],
        out_specs=pl.BlockSpec((FFN_TM, d), lambda i, j: (i, 0)),
        out_shape=jax.ShapeDtypeStruct((t, d), F32),
        scratch_shapes=[pltpu.VMEM((FFN_TM, d), BF16)],
        compiler_params=pltpu.CompilerParams(
            dimension_semantics=("parallel", "arbitrary"), vmem_limit_bytes=VMEM_LIMIT),
        name="ffn_final" if final_norm else "ffn",
    )(x, g, wg, wu, wd, fg)


INP_TM = 512


def _rope128(r, cos_t, sin_t):
    partner = pltpu.roll(r, LANES - QK_ROPE // 2, axis=1) + pltpu.roll(r, QK_ROPE // 2, axis=1)
    return r * cos_t + partner * sin_t


def _inproj_kernel(x_ref, g_ref, wu_ref, wq_ref, wkv_ref, wpe_ref, qan_ref, wqup_ref, kvan_ref,
                   wkvup_ref, qn_ref, kn_ref, cos_ref, sin_ref,
                   u_ref, q_ref, k_ref, v_ref):
    h = _rms(x_ref[...], g_ref[...]).astype(BF16)
    u_ref[...] = jnp.dot(h, wu_ref[...], preferred_element_type=F32)
    ql = jnp.dot(h, wq_ref[...], preferred_element_type=F32)
    kvl = jnp.dot(h, wkv_ref[...], preferred_element_type=F32)
    kpe = jnp.dot(h, wpe_ref[...], preferred_element_type=F32)
    q = jnp.dot(_rms(ql, qan_ref[...]).astype(BF16), wqup_ref[...], preferred_element_type=F32)
    kv = jnp.dot(_rms(kvl, kvan_ref[...]).astype(BF16), wkvup_ref[...], preferred_element_type=F32)
    cos_t = cos_ref[...]
    sin_t = sin_ref[...]
    qn = qn_ref[...]
    kn = kn_ref[...]
    kpe_ssq = jnp.sum(kpe * kpe, axis=-1, keepdims=True)
    for hd in range(N_HEADS):
        qh = q[:, hd * HEAD_PAD:(hd + 1) * HEAD_PAD]
        r = lax.rsqrt(jnp.sum(qh * qh, axis=-1, keepdims=True) * (1.0 / QK_HEAD) + EPS)
        qh = qh * r * qn
        q_ref[hd, :, :LANES] = qh[:, :LANES].astype(BF16)
        q_ref[hd, :, LANES:] = _rope128(qh[:, LANES:], cos_t, sin_t).astype(BF16)
        kno = kv[:, hd * HEAD_PAD:hd * HEAD_PAD + QK_NOPE]
        r = lax.rsqrt((jnp.sum(kno * kno, axis=-1, keepdims=True) + kpe_ssq) * (1.0 / QK_HEAD) + EPS)
        k_ref[hd, :, :LANES] = (kno * r * kn[:, :LANES]).astype(BF16)
        k_ref[hd, :, LANES:] = _rope128(kpe * r * kn[:, LANES:], cos_t, sin_t).astype(BF16)
        v_ref[hd] = kv[:, hd * HEAD_PAD + QK_NOPE:(hd + 1) * HEAD_PAD].astype(BF16)


def _inproj(x, g, wu, wq, wkv, wpe, qan, wqup, kvan, wkvup, qn, kn, cos_t, sin_t):
    t, d = x.shape
    tm = INP_TM
    row = lambda i: (i, 0)
    head_row = lambda i: (0, i, 0)
    return pl.pallas_call(
        _inproj_kernel,
        grid=(t // tm,),
        in_specs=[
            pl.BlockSpec((tm, d), row),
            _const_spec(g.shape), _const_spec(wu.shape), _const_spec(wq.shape), _const_spec(wkv.shape),
            _const_spec(wpe.shape), _const_spec(qan.shape), _const_spec(wqup.shape),
            _const_spec(kvan.shape), _const_spec(wkvup.shape), _const_spec(qn.shape), _const_spec(kn.shape),
            pl.BlockSpec((tm, LANES), row),
            pl.BlockSpec((tm, LANES), row),
        ],
        out_specs=[
            pl.BlockSpec((tm, SSM_WIDTH), row),
            pl.BlockSpec((N_HEADS, tm, HEAD_PAD), head_row),
            pl.BlockSpec((N_HEADS, tm, HEAD_PAD), head_row),
            pl.BlockSpec((N_HEADS, tm, V_HEAD), head_row),
        ],
        out_shape=[
            jax.ShapeDtypeStruct((t, SSM_WIDTH), F32),
            jax.ShapeDtypeStruct((N_HEADS, t, HEAD_PAD), BF16),
            jax.ShapeDtypeStruct((N_HEADS, t, HEAD_PAD), BF16),
            jax.ShapeDtypeStruct((N_HEADS, t, V_HEAD), BF16),
        ],
        compiler_params=pltpu.CompilerParams(
            dimension_semantics=("parallel",), vmem_limit_bytes=VMEM_LIMIT),
        name="in_proj",
    )(x, g, wu, wq, wkv, wpe, qan, wqup, kvan, wkvup, qn, kn, cos_t, sin_t)


S5_TC = 256
S5_KT = SSM_WIDTH // MXU_DIM
S5_SW = MXU_DIM // SSM_GROUP * SSM_STATE
S5_TILES = 2 * S5_SW // LANES
S5_ROWS = S5_KT * S5_TILES * SUBLANES


def _s5_kernel(u_ref, wb_ref, lr_ref, li_ref, wc_ref, d_ref, wglu_ref, bglu_ref, on_ref,
               y_ref, bu_ref, st_ref, carry_ref):
    rg = S5_TC // SUBLANES

    @pl.when(pl.program_id(1) == 0)
    def _():
        carry_ref[...] = jnp.zeros_like(carry_ref)

    u = u_ref[...]
    ub = u.astype(BF16)
    for j in range(S5_KT):
        res = jnp.dot(ub[:, j * MXU_DIM:(j + 1) * MXU_DIM], wb_ref[j], preferred_element_type=F32)
        for c in range(S5_TILES):
            row0 = (j * S5_TILES + c) * SUBLANES
            bu_ref[:, row0:row0 + SUBLANES, :] = res[:, c * LANES:(c + 1) * LANES].reshape(rg, SUBLANES, LANES)

    lam = [(lr_ref[j], li_ref[j]) for j in range(S5_KT)]
    half = S5_TILES // 2 * SUBLANES

    def step(r, carry):
        for s in range(SUBLANES):
            nxt = []
            for j in range(S5_KT):
                base = j * S5_TILES * SUBLANES + s
                idx_r = pl.ds(base, SUBLANES, stride=SUBLANES)
                idx_i = pl.ds(base + half, SUBLANES, stride=SUBLANES)
                lr, li = lam[j]
                sr, si = carry[2 * j], carry[2 * j + 1]
                nr = lr * sr - li * si + bu_ref[r, idx_r, :]
                ni = lr * si + li * sr + bu_ref[r, idx_i, :]
                st_ref[r, idx_r, :] = nr
                st_ref[r, idx_i, :] = ni
                nxt += [nr, ni]
            carry = tuple(nxt)
        return carry

    init = tuple(carry_ref[v] for v in range(2 * S5_KT))
    fin = lax.fori_loop(0, rg, step, init)
    for v in range(2 * S5_KT):
        carry_ref[v] = fin[v]

    ys = []
    for j in range(S5_KT):
        tiles = [st_ref[:, (j * S5_TILES + c) * SUBLANES:(j * S5_TILES + c + 1) * SUBLANES, :]
                 .reshape(S5_TC, LANES) for c in range(S5_TILES)]
        sj = jnp.concatenate(tiles, axis=1).astype(BF16)
        ys.append(jnp.dot(sj, wc_ref[j], preferred_element_type=F32))
    y = jnp.concatenate(ys, axis=1) + d_ref[...] * u
    y = jax.nn.gelu(y)
    y = y * jax.nn.sigmoid(jnp.dot(y.astype(BF16), wglu_ref[...], preferred_element_type=F32) + bglu_ref[...])
    y_ref[...] = _rms(y, on_ref[...]).astype(BF16)


def _s5(u, wb, lr, li, wc, d, wglu, bglu, on, batch, seq):
    nchunk = seq // S5_TC
    row = lambda b, c: (b * nchunk + c, 0)
    rg = S5_TC // SUBLANES
    return pl.pallas_call(
        _s5_kernel,
        grid=(batch, nchunk),
        in_specs=[
            pl.BlockSpec((S5_TC, SSM_WIDTH), row),
            _const_spec(wb.shape), _const_spec(lr.shape), _const_spec(li.shape), _const_spec(wc.shape),
            _const_spec(d.shape), _const_spec(wglu.shape), _const_spec(bglu.shape), _const_spec(on.shape),
        ],
        out_specs=pl.BlockSpec((S5_TC, SSM_WIDTH), row),
        out_shape=jax.ShapeDtypeStruct((batch * seq, SSM_WIDTH), BF16),
        scratch_shapes=[
            pltpu.VMEM((rg, S5_ROWS, LANES), F32),
            pltpu.VMEM((rg, S5_ROWS, LANES), F32),
            pltpu.VMEM((2 * S5_KT, SUBLANES, LANES), F32),
        ],
        compiler_params=pltpu.CompilerParams(
            dimension_semantics=("parallel", "arbitrary"), vmem_limit_bytes=VMEM_LIMIT),
        name="s5",
    )(u, wb, lr, li, wc, d, wglu, bglu, on)


ATT_T = 256


def _attn_kernel(q_ref, k_ref, v_ref, o_ref):
    i = pl.program_id(2)
    q = q_ref[...]
    scale = QK_HEAD ** -0.5

    def tile(kt, masked, carry):
        m, l, acc = carry
        start = pl.multiple_of(kt * ATT_T, ATT_T)
        k = k_ref[pl.ds(start, ATT_T), :]
        v = v_ref[pl.ds(start, ATT_T), :]
        s = lax.dot_general(q, k, (((1,), (1,)), ((), ())), preferred_element_type=F32) * scale
        if masked:
            qc = lax.broadcasted_iota(jnp.int32, (ATT_T, ATT_T), 0) // CHUNK
            kc = lax.broadcasted_iota(jnp.int32, (ATT_T, ATT_T), 1) // CHUNK
            s = jnp.where(kc <= qc, s, -jnp.inf)
        m_new = jnp.maximum(m, jnp.max(s, axis=-1, keepdims=True))
        alpha = jnp.exp(m - m_new)
        p = jnp.exp(s - m_new)
        l = alpha * l + jnp.sum(p, axis=-1, keepdims=True)
        acc = alpha * acc + jnp.dot(p.astype(BF16), v, preferred_element_type=F32)
        return m_new, l, acc

    init = (jnp.full((ATT_T, 1), -1e30, F32), jnp.zeros((ATT_T, 1), F32), jnp.zeros((ATT_T, V_HEAD), F32))
    carry = lax.fori_loop(0, i, lambda kt, c: tile(kt, False, c), init)
    _, l, acc = tile(i, True, carry)
    o_ref[...] = acc / l


def _attention(q, k, v, batch, seq):
    nq = seq // ATT_T
    return pl.pallas_call(
        _attn_kernel,
        grid=(batch, N_HEADS, nq),
        in_specs=[
            pl.BlockSpec((None, ATT_T, HEAD_PAD), lambda b, h, i: (h, b * nq + i, 0)),
            pl.BlockSpec((None, seq, HEAD_PAD), lambda b, h, i: (h, b, 0)),
            pl.BlockSpec((None, seq, V_HEAD), lambda b, h, i: (h, b, 0)),
        ],
        out_specs=pl.BlockSpec((ATT_T, V_HEAD), lambda b, h, i: (b * nq + i, h)),
        out_shape=jax.ShapeDtypeStruct((batch * seq, ATTN_WIDTH), F32),
        compiler_params=pltpu.CompilerParams(
            dimension_semantics=("parallel", "parallel", "arbitrary"), vmem_limit_bytes=VMEM_LIMIT),
        name="attention",
    )(q, k, v)


OUT_TM = 512


def _outproj_kernel(x_ref, ys_ref, ya_ref, an_ref, ws_ref, wa_ref, o_ref):
    ya = _rms(ya_ref[...], an_ref[...]).astype(BF16)
    o_ref[...] = (x_ref[...]
                  + jnp.dot(ys_ref[...], ws_ref[...], preferred_element_type=F32)
                  + jnp.dot(ya, wa_ref[...], preferred_element_type=F32))


def _outproj(x, ys, ya, an, ws, wa):
    t, d = x.shape
    tm = OUT_TM
    row = lambda i: (i, 0)
    return pl.pallas_call(
        _outproj_kernel,
        grid=(t // tm,),
        in_specs=[
            pl.BlockSpec((tm, d), row),
            pl.BlockSpec((tm, SSM_WIDTH), row),
            pl.BlockSpec((tm, ATTN_WIDTH), row),
            _const_spec(an.shape), _const_spec(ws.shape), _const_spec(wa.shape),
        ],
        out_specs=pl.BlockSpec((tm, d), row),
        out_shape=jax.ShapeDtypeStruct((t, d), F32),
        compiler_params=pltpu.CompilerParams(
            dimension_semantics=("parallel",), vmem_limit_bytes=VMEM_LIMIT),
        name="out_proj",
    )(x, ys, ya, an, ws, wa)


def _s5_params(log_step, a_re, a_im, b_re, b_im, c_re, c_im):
    g16 = MXU_DIM // SSM_GROUP
    dt = jnp.exp(log_step)[:, None]
    mag = jnp.exp(a_re * dt)
    ang = a_im * dt
    lr, li = mag * jnp.cos(ang), mag * jnp.sin(ang)
    den = a_re * a_re + a_im * a_im
    fr = ((lr - 1.0) * a_re + li * a_im) / den
    fi = (li * a_re - (lr - 1.0) * a_im) / den
    bb_r = fr[..., None] * b_re - fi[..., None] * b_im
    bb_i = fr[..., None] * b_im + fi[..., None] * b_re
    eye = jnp.eye(g16, dtype=F32)

    def bd_in(b):
        b = b.reshape(S5_KT, g16, SSM_STATE, SSM_GROUP)
        return jnp.einsum('jgnc,gh->jgchn', b, eye).reshape(S5_KT, MXU_DIM, S5_SW)

    def bd_out(c):
        c = c.reshape(S5_KT, g16, SSM_GROUP, SSM_STATE)
        return jnp.einsum('jgcn,gh->jgnhc', c, eye).reshape(S5_KT, S5_SW, MXU_DIM)

    wb = jnp.concatenate([bd_in(bb_r), bd_in(bb_i)], axis=2).astype(BF16)
    wc = jnp.concatenate([bd_out(c_re), bd_out(-c_im)], axis=1).astype(BF16)
    lam_r = lr.reshape(S5_KT, SUBLANES, LANES)
    lam_i = li.reshape(S5_KT, SUBLANES, LANES)
    return wb, lam_r, lam_i, wc


def _pad_heads(w, head, pad):
    k = w.shape[0]
    w = w.reshape(k, N_HEADS, head)
    return jnp.pad(w, ((0, 0), (0, 0), (0, pad - head))).reshape(k, N_HEADS * pad)


def kernel(x, positions, ffn1_norm, ffn1_w_gate, ffn1_w_up, ffn1_w_down, mix_norm, w_in, ssm_log_step, ssm_a_re, ssm_a_im, ssm_b_re, ssm_b_im, ssm_c_re, ssm_c_im, ssm_d, ssm_w_glu, ssm_b_glu, mla_q_a_norm, mla_w_q_up, mla_kv_a_norm, mla_w_kv_up, mla_q_norm, mla_k_norm, ssm_out_norm, attn_out_norm, w_out, ffn2_norm, ffn2_w_gate, ffn2_w_up, ffn2_w_down, final_norm):
    batch, seq, d = x.shape
    depth = w_in.shape[0]
    t = batch * seq
    xt = x.reshape(t, d)

    inv_freq = ROPE_THETA ** (-jnp.arange(0, QK_ROPE, 2, dtype=F32) / QK_ROPE)
    ang = positions.astype(F32).reshape(t, 1) * inv_freq
    cos, sin = jnp.cos(ang), jnp.sin(ang)
    zpad = jnp.zeros((t, LANES - QK_ROPE), F32)
    cos_t = jnp.concatenate([cos, cos, zpad], axis=1)
    sin_t = jnp.concatenate([-sin, sin, zpad], axis=1)

    o1 = SSM_WIDTH
    o2 = o1 + Q_LORA
    o3 = o2 + KV_LORA
    row = lambda v: v.reshape(1, -1)
    for l in range(depth):
        xt = _ffn(xt, row(ffn1_norm[l]), ffn1_w_gate[l].astype(BF16), ffn1_w_up[l].astype(BF16),
                  ffn1_w_down[l].astype(BF16), row(final_norm[l]), final_norm=False)

        w = w_in[l].astype(BF16)
        wpe = jnp.pad(w[:, o3:], ((0, 0), (0, LANES - QK_ROPE)))
        qn = jnp.pad(row(mla_q_norm[l]), ((0, 0), (0, HEAD_PAD - QK_HEAD)))
        kn = jnp.pad(row(mla_k_norm[l]), ((0, 0), (0, HEAD_PAD - QK_HEAD)))
        u, q, k, v = _inproj(
            xt, row(mix_norm[l]), w[:, :o1], w[:, o1:o2], w[:, o2:o3], wpe,
            row(mla_q_a_norm[l]), _pad_heads(mla_w_q_up[l], QK_HEAD, HEAD_PAD).astype(BF16),
            row(mla_kv_a_norm[l]), mla_w_kv_up[l].astype(BF16), qn, kn, cos_t, sin_t)

        wb, lam_r, lam_i, wc = _s5_params(ssm_log_step[l], ssm_a_re[l], ssm_a_im[l], ssm_b_re[l],
                                          ssm_b_im[l], ssm_c_re[l], ssm_c_im[l])
        ys = _s5(u, wb, lam_r, lam_i, wc, row(ssm_d[l]), ssm_w_glu[l].astype(BF16), row(ssm_b_glu[l]),
                 row(ssm_out_norm[l]), batch, seq)

        ya = _attention(q, k, v, batch, seq)

        wo = w_out[l].astype(BF16)
        xt = _outproj(xt, ys, ya, row(attn_out_norm[l]), wo[:SSM_WIDTH], wo[SSM_WIDTH:])

        xt = _ffn(xt, row(ffn2_norm[l]), ffn2_w_gate[l].astype(BF16), ffn2_w_up[l].astype(BF16),
                  ffn2_w_down[l].astype(BF16), row(final_norm[l]), final_norm=True)
    return xt.reshape(batch, seq, d)
```

```python
import functools
import math

import jax
import jax.numpy as jnp
from jax import lax
from jax.experimental import pallas as pl
from jax.experimental.pallas import tpu as pltpu

D_MODEL = 2048
CHUNK = 64
SSM_WIDTH = 1024
SSM_GROUP = 16
SSM_GROUPS = 64
SSM_STATE = 64
N_HEADS = 8
QK_NOPE = 128
QK_ROPE = 64
QK_HEAD = QK_NOPE + QK_ROPE
V_HEAD = 128
ATTN_WIDTH = N_HEADS * V_HEAD
Q_LORA = 512
KV_LORA = 256
D_FF = 5632
ROPE_THETA = 10000.0
EPS = 1e-6

LANES = 128
SUBLANES = 8
MXU_DIM = 256
HEAD_PAD = 2 * LANES
VMEM_LIMIT = 56 * 1024 * 1024

BF16 = jnp.bfloat16
F32 = jnp.float32


def _rms(xf, g):
    return xf * lax.rsqrt(jnp.mean(xf * xf, axis=-1, keepdims=True) + EPS) * g


def _const_spec(shape):
    nd = len(shape)
    return pl.BlockSpec(shape, lambda *_: (0,) * nd, pipeline_mode=pl.Buffered(1))


FFN_TM = 1024
FFN_TF = 512


def _ffn_kernel(x_ref, g_ref, wg_ref, wu_ref, wd_ref, fg_ref, o_ref, h_ref, *, final_norm):
    j = pl.program_id(1)

    @pl.when(j == 0)
    def _():
        h_ref[...] = _rms(x_ref[...], g_ref[...]).astype(BF16)
        o_ref[...] = jnp.zeros_like(o_ref)

    h = h_ref[...]
    gate = jnp.dot(h, wg_ref[...], preferred_element_type=F32)
    up = jnp.dot(h, wu_ref[...], preferred_element_type=F32)
    act = (gate * jax.nn.sigmoid(gate) * up).astype(BF16)
    o_ref[...] += jnp.dot(act, wd_ref[...], preferred_element_type=F32)

    @pl.when(j == pl.num_programs(1) - 1)
    def _():
        y = x_ref[...] + 0.5 * o_ref[...]
        if final_norm:
            y = _rms(y, fg_ref[...])
        o_ref[...] = y


def _ffn(x, g, wg, wu, wd, fg, final_norm):
    t, d = x.shape
    dff = wg.shape[1]
    grid = (t // FFN_TM, dff // FFN_TF)
    return pl.pallas_call(
        functools.partial(_ffn_kernel, final_norm=final_norm),
        grid=grid,
        in_specs=[
            pl.BlockSpec((FFN_TM, d), lambda i, j: (i, 0), pipeline_mode=pl.Buffered(1)),
            pl.BlockSpec((1, d), lambda i, j: (0, 0)),
            pl.BlockSpec((d, FFN_TF), lambda i, j: (0, j)),
            pl.BlockSpec((d, FFN_TF), lambda i, j: (0, j)),
            pl.BlockSpec((FFN_TF, d), lambda i, j: (j, 0)),
            pl.BlockSpec((1, d), lambda i, j: (0, 0)),
        ],
        out_specs=pl.BlockSpec((FFN_TM, d), lambda i, j: (i, 0)),
        out_shape=jax.ShapeDtypeStruct((t, d), F32),
        scratch_shapes=[pltpu.VMEM((FFN_TM, d), BF16)],
        compiler_params=pltpu.CompilerParams(
            dimension_semantics=("parallel", "arbitrary"), vmem_limit_bytes=VMEM_LIMIT),
        name="ffn_final" if final_norm else "ffn",
    )(x, g, wg, wu, wd, fg)


INP_TM = 512


def _rope128(r, cos_t, sin_t):
    partner = pltpu.roll(r, LANES - QK_ROPE // 2, axis=1) + pltpu.roll(r, QK_ROPE // 2, axis=1)
    return r * cos_t + partner * sin_t


def _inproj_kernel(x_ref, g_ref, wu_ref, wq_ref, wkv_ref, wpe_ref, qan_ref, wqup_ref, kvan_ref,
                   wkvup_ref, qn_ref, kn_ref, cos_ref, sin_ref,
                   u_ref, q_ref, k_ref, v_ref):
    h = _rms(x_ref[...], g_ref[...]).astype(BF16)
    u_ref[...] = jnp.dot(h, wu_ref[...], preferred_element_type=F32)
    ql = jnp.dot(h, wq_ref[...], preferred_element_type=F32)
    kvl = jnp.dot(h, wkv_ref[...], preferred_element_type=F32)
    kpe = jnp.dot(h, wpe_ref[...], preferred_element_type=F32)
    q = jnp.dot(_rms(ql, qan_ref[...]).astype(BF16), wqup_ref[...], preferred_element_type=F32)
    kv = jnp.dot(_rms(kvl, kvan_ref[...]).astype(BF16), wkvup_ref[...], preferred_element_type=F32)
    cos_t = cos_ref[...]
    sin_t = sin_ref[...]
    qn = qn_ref[...]
    kn = kn_ref[...]
    kpe_ssq = jnp.sum(kpe * kpe, axis=-1, keepdims=True)
    for hd in range(N_HEADS):
        qh = q[:, hd * HEAD_PAD:(hd + 1) * HEAD_PAD]
        r = lax.rsqrt(jnp.sum(qh * qh, axis=-1, keepdims=True) * (1.0 / QK_HEAD) + EPS)
        qh = qh * r * qn
        q_ref[hd, :, :LANES] = qh[:, :LANES].astype(BF16)
        q_ref[hd, :, LANES:] = _rope128(qh[:, LANES:], cos_t, sin_t).astype(BF16)
        kno = kv[:, hd * HEAD_PAD:hd * HEAD_PAD + QK_NOPE]
        r = lax.rsqrt((jnp.sum(kno * kno, axis=-1, keepdims=True) + kpe_ssq) * (1.0 / QK_HEAD) + EPS)
        k_ref[hd, :, :LANES] = (kno * r * kn[:, :LANES]).astype(BF16)
        k_ref[hd, :, LANES:] = _rope128(kpe * r * kn[:, LANES:], cos_t, sin_t).astype(BF16)
        v_ref[hd] = kv[:, hd * HEAD_PAD + QK_NOPE:(hd + 1) * HEAD_PAD].astype(BF16)


def _inproj(x, g, wu, wq, wkv, wpe, qan, wqup, kvan, wkvup, qn, kn, cos_t, sin_t):
    t, d = x.shape
    tm = INP_TM
    row = lambda i: (i, 0)
    head_row = lambda i: (0, i, 0)
    return pl.pallas_call(
        _inproj_kernel,
        grid=(t // tm,),
        in_specs=[
            pl.BlockSpec((tm, d), row),
            _const_spec(g.shape), _const_spec(wu.shape), _const_spec(wq.shape), _const_spec(wkv.shape),
            _const_spec(wpe.shape), _const_spec(qan.shape), _const_spec(wqup.shape),
            _const_spec(kvan.shape), _const_spec(wkvup.shape), _const_spec(qn.shape), _const_spec(kn.shape),
            pl.BlockSpec((tm, LANES), row),
            pl.BlockSpec((tm, LANES), row),
        ],
        out_specs=[
            pl.BlockSpec((tm, SSM_WIDTH), row),
            pl.BlockSpec((N_HEADS, tm, HEAD_PAD), head_row),
            pl.BlockSpec((N_HEADS, tm, HEAD_PAD), head_row),
            pl.BlockSpec((N_HEADS, tm, V_HEAD), head_row),
        ],
        out_shape=[
            jax.ShapeDtypeStruct((t, SSM_WIDTH), F32),
            jax.ShapeDtypeStruct((N_HEADS, t, HEAD_PAD), BF16),
            jax.ShapeDtypeStruct((N_HEADS, t, HEAD_PAD), BF16),
            jax.ShapeDtypeStruct((N_HEADS, t, V_HEAD), BF16),
        ],
        compiler_params=pltpu.CompilerParams(
            dimension_semantics=("parallel",), vmem_limit_bytes=VMEM_LIMIT),
        name="in_proj",
    )(x, g, wu, wq, wkv, wpe, qan, wqup, kvan, wkvup, qn, kn, cos_t, sin_t)


S5_TC = 256
S5_KT = SSM_WIDTH // MXU_DIM
S5_SW = MXU_DIM // SSM_GROUP * SSM_STATE
S5_TILES = 2 * S5_SW // LANES
S5_ROWS = S5_KT * S5_TILES * SUBLANES


def _s5_kernel(u_ref, wb_ref, lr_ref, li_ref, wc_ref, d_ref, wglu_ref, bglu_ref, on_ref,
               y_ref, bu_ref, st_ref, carry_ref):
    rg = S5_TC // SUBLANES

    @pl.when(pl.program_id(1) == 0)
    def _():
        carry_ref[...] = jnp.zeros_like(carry_ref)

    u = u_ref[...]
    ub = u.astype(BF16)
    for j in range(S5_KT):
        res = jnp.dot(ub[:, j * MXU_DIM:(j + 1) * MXU_DIM], wb_ref[j], preferred_element_type=F32)
        for c in range(S5_TILES):
            row0 = (j * S5_TILES + c) * SUBLANES
            bu_ref[:, row0:row0 + SUBLANES, :] = res[:, c * LANES:(c + 1) * LANES].reshape(rg, SUBLANES, LANES)

    lam = [(lr_ref[j], li_ref[j]) for j in range(S5_KT)]
    half = S5_TILES // 2 * SUBLANES

    def step(r, carry):
        for s in range(SUBLANES):
            nxt = []
            for j in range(S5_KT):
                base = j * S5_TILES * SUBLANES + s
                idx_r = pl.ds(base, SUBLANES, stride=SUBLANES)
                idx_i = pl.ds(base + half, SUBLANES, stride=SUBLANES)
                lr, li = lam[j]
                sr, si = carry[2 * j], carry[2 * j + 1]
                nr = lr * sr - li * si + bu_ref[r, idx_r, :]
                ni = lr * si + li * sr + bu_ref[r, idx_i, :]
                st_ref[r, idx_r, :] = nr
                st_ref[r, idx_i, :] = ni
                nxt += [nr, ni]
            carry = tuple(nxt)
        return carry

    init = tuple(carry_ref[v] for v in range(2 * S5_KT))
    fin = lax.fori_loop(0, rg, step, init)
    for v in range(2 * S5_KT):
        carry_ref[v] = fin[v]

    ys = []
    for j in range(S5_KT):
        tiles = [st_ref[:, (j * S5_TILES + c) * SUBLANES:(j * S5_TILES + c + 1) * SUBLANES, :]
                 .reshape(S5_TC, LANES) for c in range(S5_TILES)]
        sj = jnp.concatenate(tiles, axis=1).astype(BF16)
        ys.append(jnp.dot(sj, wc_ref[j], preferred_element_type=F32))
    y = jnp.concatenate(ys, axis=1) + d_ref[...] * u
    y = jax.nn.gelu(y)
    y = y * jax.nn.sigmoid(jnp.dot(y.astype(BF16), wglu_ref[...], preferred_element_type=F32) + bglu_ref[...])
    y_ref[...] = _rms(y, on_ref[...]).astype(BF16)


def _s5(u, wb, lr, li, wc, d, wglu, bglu, on, batch, seq):
    nchunk = seq // S5_TC
    row = lambda b, c: (b * nchunk + c, 0)
    rg = S5_TC // SUBLANES
    return pl.pallas_call(
        _s5_kernel,
        grid=(batch, nchunk),
        in_specs=[
            pl.BlockSpec((S5_TC, SSM_WIDTH), row),
            _const_spec(wb.shape), _const_spec(lr.shape), _const_spec(li.shape), _const_spec(wc.shape),
            _const_spec(d.shape), _const_spec(wglu.shape), _const_spec(bglu.shape), _const_spec(on.shape),
        ],
        out_specs=pl.BlockSpec((S5_TC, SSM_WIDTH), row),
        out_shape=jax.ShapeDtypeStruct((batch * seq, SSM_WIDTH), BF16),
        scratch_shapes=[
            pltpu.VMEM((rg, S5_ROWS, LANES), F32),
            pltpu.VMEM((rg, S5_ROWS, LANES), F32),
            pltpu.VMEM((2 * S5_KT, SUBLANES, LANES), F32),
        ],
        compiler_params=pltpu.CompilerParams(
            dimension_semantics=("parallel", "arbitrary"), vmem_limit_bytes=VMEM_LIMIT),
        name="s5",
    )(u, wb, lr, li, wc, d, wglu, bglu, on)


ATT_TQ = 512


def _attn_kernel(q_ref, k_ref, v_ref, bias_ref, o_ref):
    seq = q_ref.shape[0]
    scale = QK_HEAD ** -0.5
    nt = (((1,), (1,)), ((), ()))
    bias = bias_ref[...]
    for qi in range(seq // ATT_TQ):
        r0 = qi * ATT_TQ
        q = q_ref[r0:r0 + ATT_TQ, :]
        s_d = lax.dot_general(q, k_ref[r0:r0 + ATT_TQ, :], nt, preferred_element_type=F32) * scale + bias
        m = jnp.max(s_d, axis=-1, keepdims=True)
        if qi > 0:
            s_o = lax.dot_general(q, k_ref[:r0, :], nt, preferred_element_type=F32) * scale
            m = jnp.maximum(m, jnp.max(s_o, axis=-1, keepdims=True))
            p_o = jnp.exp(s_o - m)
            l = jnp.sum(p_o, axis=-1, keepdims=True)
            acc = jnp.dot(p_o.astype(BF16), v_ref[:r0, :], preferred_element_type=F32)
        p_d = jnp.exp(s_d - m)
        l_d = jnp.sum(p_d, axis=-1, keepdims=True)
        acc_d = jnp.dot(p_d.astype(BF16), v_ref[r0:r0 + ATT_TQ, :], preferred_element_type=F32)
        if qi > 0:
            l_d = l_d + l
            acc_d = acc_d + acc
        o_ref[r0:r0 + ATT_TQ, :] = acc_d / l_d


def _attention(q, k, v, batch, seq):
    qc = lax.broadcasted_iota(jnp.int32, (ATT_TQ, ATT_TQ), 0) // CHUNK
    kc = lax.broadcasted_iota(jnp.int32, (ATT_TQ, ATT_TQ), 1) // CHUNK
    bias = jnp.where(kc <= qc, 0.0, -jnp.inf).astype(F32)
    return pl.pallas_call(
        _attn_kernel,
        grid=(batch, N_HEADS),
        in_specs=[
            pl.BlockSpec((None, seq, HEAD_PAD), lambda b, h: (h, b, 0)),
            pl.BlockSpec((None, seq, HEAD_PAD), lambda b, h: (h, b, 0)),
            pl.BlockSpec((None, seq, V_HEAD), lambda b, h: (h, b, 0)),
            _const_spec(bias.shape),
        ],
        out_specs=pl.BlockSpec((seq, V_HEAD), lambda b, h: (b, h)),
        out_shape=jax.ShapeDtypeStruct((batch * seq, ATTN_WIDTH), F32),
        compiler_params=pltpu.CompilerParams(
            dimension_semantics=("parallel", "parallel"), vmem_limit_bytes=VMEM_LIMIT),
        name="attention",
    )(q, k, v, bias)


OUT_TM = 512


def _outproj_kernel(x_ref, ys_ref, ya_ref, an_ref, ws_ref, wa_ref, o_ref):
    ya = _rms(ya_ref[...], an_ref[...]).astype(BF16)
    o_ref[...] = (x_ref[...]
                  + jnp.dot(ys_ref[...], ws_ref[...], preferred_element_type=F32)
                  + jnp.dot(ya, wa_ref[...], preferred_element_type=F32))


def _outproj(x, ys, ya, an, ws, wa):
    t, d = x.shape
    tm = OUT_TM
    row = lambda i: (i, 0)
    return pl.pallas_call(
        _outproj_kernel,
        grid=(t // tm,),
        in_specs=[
            pl.BlockSpec((tm, d), row),
            pl.BlockSpec((tm, SSM_WIDTH), row),
            pl.BlockSpec((tm, ATTN_WIDTH), row),
            _const_spec(an.shape), _const_spec(ws.shape), _const_spec(wa.shape),
        ],
        out_specs=pl.BlockSpec((tm, d), row),
        out_shape=jax.ShapeDtypeStruct((t, d), F32),
        compiler_params=pltpu.CompilerParams(
            dimension_semantics=("parallel",), vmem_limit_bytes=VMEM_LIMIT),
        name="out_proj",
    )(x, ys, ya, an, ws, wa)


def _s5_params(log_step, a_re, a_im, b_re, b_im, c_re, c_im):
    g16 = MXU_DIM // SSM_GROUP
    dt = jnp.exp(log_step)[:, None]
    mag = jnp.exp(a_re * dt)
    ang = a_im * dt
    lr, li = mag * jnp.cos(ang), mag * jnp.sin(ang)
    den = a_re * a_re + a_im * a_im
    fr = ((lr - 1.0) * a_re + li * a_im) / den
    fi = (li * a_re - (lr - 1.0) * a_im) / den
    bb_r = fr[..., None] * b_re - fi[..., None] * b_im
    bb_i = fr[..., None] * b_im + fi[..., None] * b_re
    eye = jnp.eye(g16, dtype=F32)

    def bd_in(b):
        b = b.reshape(S5_KT, g16, SSM_STATE, SSM_GROUP)
        return jnp.einsum('jgnc,gh->jgchn', b, eye).reshape(S5_KT, MXU_DIM, S5_SW)

    def bd_out(c):
        c = c.reshape(S5_KT, g16, SSM_GROUP, SSM_STATE)
        return jnp.einsum('jgcn,gh->jgnhc', c, eye).reshape(S5_KT, S5_SW, MXU_DIM)

    wb = jnp.concatenate([bd_in(bb_r), bd_in(bb_i)], axis=2).astype(BF16)
    wc = jnp.concatenate([bd_out(c_re), bd_out(-c_im)], axis=1).astype(BF16)
    lam_r = lr.reshape(S5_KT, SUBLANES, LANES)
    lam_i = li.reshape(S5_KT, SUBLANES, LANES)
    return wb, lam_r, lam_i, wc


def _pad_heads(w, head, pad):
    k = w.shape[0]
    w = w.reshape(k, N_HEADS, head)
    return jnp.pad(w, ((0, 0), (0, 0), (0, pad - head))).reshape(k, N_HEADS * pad)


def kernel(x, positions, ffn1_norm, ffn1_w_gate, ffn1_w_up, ffn1_w_down, mix_norm, w_in, ssm_log_step, ssm_a_re, ssm_a_im, ssm_b_re, ssm_b_im, ssm_c_re, ssm_c_im, ssm_d, ssm_w_glu, ssm_b_glu, mla_q_a_norm, mla_w_q_up, mla_kv_a_norm, mla_w_kv_up, mla_q_norm, mla_k_norm, ssm_out_norm, attn_out_norm, w_out, ffn2_norm, ffn2_w_gate, ffn2_w_up, ffn2_w_down, final_norm):
    batch, seq, d = x.shape
    depth = w_in.shape[0]
    t = batch * seq
    xt = x.reshape(t, d)

    inv_freq = ROPE_THETA ** (-jnp.arange(0, QK_ROPE, 2, dtype=F32) / QK_ROPE)
    ang = positions.astype(F32).reshape(t, 1) * inv_freq
    cos, sin = jnp.cos(ang), jnp.sin(ang)
    zpad = jnp.zeros((t, LANES - QK_ROPE), F32)
    cos_t = jnp.concatenate([cos, cos, zpad], axis=1)
    sin_t = jnp.concatenate([-sin, sin, zpad], axis=1)

    o1 = SSM_WIDTH
    o2 = o1 + Q_LORA
    o3 = o2 + KV_LORA
    row = lambda v: v.reshape(1, -1)
    for l in range(depth):
        xt = _ffn(xt, row(ffn1_norm[l]), ffn1_w_gate[l].astype(BF16), ffn1_w_up[l].astype(BF16),
                  ffn1_w_down[l].astype(BF16), row(final_norm[l]), final_norm=False)

        w = w_in[l].astype(BF16)
        wpe = jnp.pad(w[:, o3:], ((0, 0), (0, LANES - QK_ROPE)))
        qn = jnp.pad(row(mla_q_norm[l]), ((0, 0), (0, HEAD_PAD - QK_HEAD)))
        kn = jnp.pad(row(mla_k_norm[l]), ((0, 0), (0, HEAD_PAD - QK_HEAD)))
        u, q, k, v = _inproj(
            xt, row(mix_norm[l]), w[:, :o1], w[:, o1:o2], w[:, o2:o3], wpe,
            row(mla_q_a_norm[l]), _pad_heads(mla_w_q_up[l], QK_HEAD, HEAD_PAD).astype(BF16),
            row(mla_kv_a_norm[l]), mla_w_kv_up[l].astype(BF16), qn, kn, cos_t, sin_t)

        wb, lam_r, lam_i, wc = _s5_params(ssm_log_step[l], ssm_a_re[l], ssm_a_im[l], ssm_b_re[l],
                                          ssm_b_im[l], ssm_c_re[l], ssm_c_im[l])
        ys = _s5(u, wb, lam_r, lam_i, wc, row(ssm_d[l]), ssm_w_glu[l].astype(BF16), row(ssm_b_glu[l]),
                 row(ssm_out_norm[l]), batch, seq)

        ya = _attention(q, k, v, batch, seq)

        wo = w_out[l].astype(BF16)
        xt = _outproj(xt, ys, ya, row(attn_out_norm[l]), wo[:SSM_WIDTH], wo[SSM_WIDTH:])

        xt = _ffn(xt, row(ffn2_norm[l]), ffn2_w_gate[l].astype(BF16), ffn2_w_up[l].astype(BF16),
                  ffn2_w_down[l].astype(BF16), row(final_norm[l]), final_norm=True)
    return xt.reshape(batch, seq, d)
```

```python
import functools
import math

import jax
import jax.numpy as jnp
from jax import lax
from jax.experimental import pallas as pl
from jax.experimental.pallas import tpu as pltpu

D_MODEL = 2048
CHUNK = 64
SSM_WIDTH = 1024
SSM_GROUP = 16
SSM_GROUPS = 64
SSM_STATE = 64
N_HEADS = 8
QK_NOPE = 128
QK_ROPE = 64
QK_HEAD = QK_NOPE + QK_ROPE
V_HEAD = 128
ATTN_WIDTH = N_HEADS * V_HEAD
Q_LORA = 512
KV_LORA = 256
D_FF = 5632
ROPE_THETA = 10000.0
EPS = 1e-6

LANES = 128
SUBLANES = 8
MXU_DIM = 256
HEAD_PAD = 2 * LANES
VMEM_LIMIT = 56 * 1024 * 1024

BF16 = jnp.bfloat16
F32 = jnp.float32


def _rms(xf, g):
    return xf * lax.rsqrt(jnp.mean(xf * xf, axis=-1, keepdims=True) + EPS) * g


def _const_spec(shape):
    nd = len(shape)
    return pl.BlockSpec(shape, lambda *_: (0,) * nd, pipeline_mode=pl.Buffered(1))


FFN_TM = 1024
FFN_TF = 256


def _ffn_kernel(x_ref, g_ref, wg_ref, wu_ref, wd_ref, fg_ref, o_ref, h_ref, *, final_norm):
    j = pl.program_id(1)

    @pl.when(j == 0)
    def _():
        h_ref[...] = _rms(x_ref[...], g_ref[...]).astype(BF16)
        o_ref[...] = jnp.zeros_like(o_ref)

    h = h_ref[...]
    gate = jnp.dot(h, wg_ref[...].astype(BF16), preferred_element_type=F32)
    up = jnp.dot(h, wu_ref[...].astype(BF16), preferred_element_type=F32)
    act = (gate * jax.nn.sigmoid(gate) * up).astype(BF16)
    o_ref[...] += jnp.dot(act, wd_ref[...].astype(BF16), preferred_element_type=F32)

    @pl.when(j == pl.num_programs(1) - 1)
    def _():
        y = x_ref[...] + 0.5 * o_ref[...]
        if final_norm:
            y = _rms(y, fg_ref[...])
        o_ref[...] = y


def _ffn(x, g, wg, wu, wd, fg, final_norm):
    t, d = x.shape
    dff = wg.shape[1]
    grid = (t // FFN_TM, dff // FFN_TF)
    return pl.pallas_call(
        functools.partial(_ffn_kernel, final_norm=final_norm),
        grid=grid,
        in_specs=[
            pl.BlockSpec((FFN_TM, d), lambda i, j: (i, 0), pipeline_mode=pl.Buffered(1)),
            pl.BlockSpec((1, d), lambda i, j: (0, 0)),
            pl.BlockSpec((d, FFN_TF), lambda i, j: (0, j)),
            pl.BlockSpec((d, FFN_TF), lambda i, j: (0, j)),
            pl.BlockSpec((FFN_TF, d), lambda i, j: (j, 0)),
            pl.BlockSpec((1, d), lambda i, j: (0, 0)),
        ],
        out_specs=pl.BlockSpec((FFN_TM, d), lambda i, j: (i, 0)),
        out_shape=jax.ShapeDtypeStruct((t, d), F32),
        scratch_shapes=[pltpu.VMEM((FFN_TM, d), BF16)],
        compiler_params=pltpu.CompilerParams(
            dimension_semantics=("parallel", "arbitrary"), vmem_limit_bytes=VMEM_LIMIT),
        name="ffn_final" if final_norm else "ffn",
    )(x, g, wg, wu, wd, fg)


INP_TM = 512
INP_SUB = 2


def _rope_dup(r, cos_t, sin_t):
    return r * cos_t + pltpu.roll(r, LANES - QK_ROPE // 2, axis=1) * sin_t


def _inproj_kernel(x_ref, g_ref, w_ref, qan_ref, wqup_ref, kvan_ref, wkvup_ref, qn_ref, kn_ref,
                   cos_ref, sin_ref, u_ref, q_ref, k_ref, v_ref):
    sub = INP_TM // INP_SUB
    inv_head = 1.0 / QK_HEAD
    for sb in range(INP_SUB):
        rows = slice(sb * sub, (sb + 1) * sub)
        h = _rms(x_ref[rows, :], g_ref[...]).astype(BF16)
        u_ref[rows, :] = jnp.dot(h, w_ref[:, :SSM_WIDTH], preferred_element_type=F32)
        lat = jnp.dot(h, w_ref[:, SSM_WIDTH:], preferred_element_type=F32)
        ql = lat[:, :Q_LORA]
        kvl = lat[:, Q_LORA:Q_LORA + KV_LORA]
        kpe = lat[:, Q_LORA + KV_LORA:]
        q = jnp.dot(_rms(ql, qan_ref[...]).astype(BF16), wqup_ref[...], preferred_element_type=F32)
        kv = jnp.dot(_rms(kvl, kvan_ref[...]).astype(BF16), wkvup_ref[...], preferred_element_type=F32)
        cos_t = cos_ref[rows, :]
        sin_t = sin_ref[rows, :]
        qn = qn_ref[...]
        kn = kn_ref[...]
        kpe_rot = _rope_dup(kpe * kn[:, LANES:], cos_t, sin_t)
        kpe_ssq = 0.5 * jnp.sum(kpe * kpe, axis=-1, keepdims=True)
        for hd in range(N_HEADS):
            qno = q[:, hd * HEAD_PAD:hd * HEAD_PAD + QK_NOPE]
            qro = q[:, hd * HEAD_PAD + QK_NOPE:(hd + 1) * HEAD_PAD]
            r = lax.rsqrt(jnp.sum(qno * qno + 0.5 * (qro * qro), axis=-1, keepdims=True) * inv_head + EPS)
            q_ref[hd, rows, :LANES] = (qno * r * qn[:, :LANES]).astype(BF16)
            q_ref[hd, rows, LANES:] = _rope_dup(qro * r * qn[:, LANES:], cos_t, sin_t).astype(BF16)
            kno = kv[:, hd * HEAD_PAD:hd * HEAD_PAD + QK_NOPE]
            r = lax.rsqrt((jnp.sum(kno * kno, axis=-1, keepdims=True) + kpe_ssq) * inv_head + EPS)
            k_ref[hd, rows, :LANES] = (kno * r * kn[:, :LANES]).astype(BF16)
            k_ref[hd, rows, LANES:] = (kpe_rot * r).astype(BF16)
            v_ref[hd, rows, :] = kv[:, hd * HEAD_PAD + QK_NOPE:(hd + 1) * HEAD_PAD].astype(BF16)


def _inproj(x, g, w, qan, wqup, kvan, wkvup, qn, kn, cos_t, sin_t):
    t, d = x.shape
    tm = INP_TM
    row = lambda i: (i, 0)
    head_row = lambda i: (0, i, 0)
    return pl.pallas_call(
        _inproj_kernel,
        grid=(t // tm,),
        in_specs=[
            pl.BlockSpec((tm, d), row),
            _const_spec(g.shape), _const_spec(w.shape), _const_spec(qan.shape), _const_spec(wqup.shape),
            _const_spec(kvan.shape), _const_spec(wkvup.shape), _const_spec(qn.shape), _const_spec(kn.shape),
            pl.BlockSpec((tm, LANES), row),
            pl.BlockSpec((tm, LANES), row),
        ],
        out_specs=[
            pl.BlockSpec((tm, SSM_WIDTH), row),
            pl.BlockSpec((N_HEADS, tm, HEAD_PAD), head_row),
            pl.BlockSpec((N_HEADS, tm, HEAD_PAD), head_row),
            pl.BlockSpec((N_HEADS, tm, V_HEAD), head_row),
        ],
        out_shape=[
            jax.ShapeDtypeStruct((t, SSM_WIDTH), F32),
            jax.ShapeDtypeStruct((N_HEADS, t, HEAD_PAD), BF16),
            jax.ShapeDtypeStruct((N_HEADS, t, HEAD_PAD), BF16),
            jax.ShapeDtypeStruct((N_HEADS, t, V_HEAD), BF16),
        ],
        compiler_params=pltpu.CompilerParams(
            dimension_semantics=("parallel",), vmem_limit_bytes=VMEM_LIMIT),
        name="in_proj",
    )(x, g, w, qan, wqup, kvan, wkvup, qn, kn, cos_t, sin_t)


S5_TC = 256
S5_KT = SSM_WIDTH // MXU_DIM
S5_SW = MXU_DIM // SSM_GROUP * SSM_STATE
S5_TILES = 2 * S5_SW // LANES
S5_ROWS = S5_KT * S5_TILES * SUBLANES


def _s5_kernel(u_ref, wb_ref, lr_ref, li_ref, wc_ref, d_ref, wglu_ref, bglu_ref, on_ref,
               y_ref, bu_ref, st_ref, carry_ref):
    rg = S5_TC // SUBLANES

    @pl.when(pl.program_id(1) == 0)
    def _():
        carry_ref[...] = jnp.zeros_like(carry_ref)

    u = u_ref[...]
    ub = u.astype(BF16)
    for j in range(S5_KT):
        res = jnp.dot(ub[:, j * MXU_DIM:(j + 1) * MXU_DIM], wb_ref[j], preferred_element_type=F32)
        for c in range(S5_TILES):
            row0 = (j * S5_TILES + c) * SUBLANES
            bu_ref[:, row0:row0 + SUBLANES, :] = res[:, c * LANES:(c + 1) * LANES].reshape(rg, SUBLANES, LANES)

    lam = [(lr_ref[j], li_ref[j]) for j in range(S5_KT)]
    half = S5_TILES // 2 * SUBLANES

    def step(r, carry):
        for s in range(SUBLANES):
            nxt = []
            for j in range(S5_KT):
                base = j * S5_TILES * SUBLANES + s
                idx_r = pl.ds(base, SUBLANES, stride=SUBLANES)
                idx_i = pl.ds(base + half, SUBLANES, stride=SUBLANES)
                lr, li = lam[j]
                sr, si = carry[2 * j], carry[2 * j + 1]
                nr = lr * sr - li * si + bu_ref[r, idx_r, :]
                ni = lr * si + li * sr + bu_ref[r, idx_i, :]
                st_ref[r, idx_r, :] = nr
                st_ref[r, idx_i, :] = ni
                nxt += [nr, ni]
            carry = tuple(nxt)
        return carry

    init = tuple(carry_ref[v] for v in range(2 * S5_KT))
    fin = lax.fori_loop(0, rg, step, init)
    for v in range(2 * S5_KT):
        carry_ref[v] = fin[v]

    ys = []
    for j in range(S5_KT):
        tiles = [st_ref[:, (j * S5_TILES + c) * SUBLANES:(j * S5_TILES + c + 1) * SUBLANES, :]
                 .reshape(S5_TC, LANES) for c in range(S5_TILES)]
        sj = jnp.concatenate(tiles, axis=1).astype(BF16)
        ys.append(jnp.dot(sj, wc_ref[j], preferred_element_type=F32))
    y = jnp.concatenate(ys, axis=1) + d_ref[...] * u
    y = jax.nn.gelu(y)
    y = y * jax.nn.sigmoid(jnp.dot(y.astype(BF16), wglu_ref[...], preferred_element_type=F32) + bglu_ref[...])
    y_ref[...] = _rms(y, on_ref[...]).astype(BF16)


def _s5(u, wb, lr, li, wc, d, wglu, bglu, on, batch, seq):
    nchunk = seq // S5_TC
    row = lambda b, c: (b * nchunk + c, 0)
    rg = S5_TC // SUBLANES
    return pl.pallas_call(
        _s5_kernel,
        grid=(batch, nchunk),
        in_specs=[
            pl.BlockSpec((S5_TC, SSM_WIDTH), row),
            _const_spec(wb.shape), _const_spec(lr.shape), _const_spec(li.shape), _const_spec(wc.shape),
            _const_spec(d.shape), _const_spec(wglu.shape), _const_spec(bglu.shape), _const_spec(on.shape),
        ],
        out_specs=pl.BlockSpec((S5_TC, SSM_WIDTH), row),
        out_shape=jax.ShapeDtypeStruct((batch * seq, SSM_WIDTH), BF16),
        scratch_shapes=[
            pltpu.VMEM((rg, S5_ROWS, LANES), F32),
            pltpu.VMEM((rg, S5_ROWS, LANES), F32),
            pltpu.VMEM((2 * S5_KT, SUBLANES, LANES), F32),
        ],
        compiler_params=pltpu.CompilerParams(
            dimension_semantics=("parallel", "arbitrary"), vmem_limit_bytes=VMEM_LIMIT),
        name="s5",
    )(u, wb, lr, li, wc, d, wglu, bglu, on)


ATT_TQ = 512


def _attn_kernel(q_ref, k_ref, v_ref, bias_ref, o_ref):
    seq = q_ref.shape[0]
    scale = QK_HEAD ** -0.5
    nt = (((1,), (1,)), ((), ()))
    bias = bias_ref[...]
    for qi in range(seq // ATT_TQ):
        r0 = qi * ATT_TQ
        q = q_ref[r0:r0 + ATT_TQ, :]
        s_d = lax.dot_general(q, k_ref[r0:r0 + ATT_TQ, :], nt, preferred_element_type=F32) * scale + bias
        m = jnp.max(s_d, axis=-1, keepdims=True)
        if qi > 0:
            s_o = lax.dot_general(q, k_ref[:r0, :], nt, preferred_element_type=F32) * scale
            m = jnp.maximum(m, jnp.max(s_o, axis=-1, keepdims=True))
            p_o = jnp.exp(s_o - m)
            l = jnp.sum(p_o, axis=-1, keepdims=True)
            acc = jnp.dot(p_o.astype(BF16), v_ref[:r0, :], preferred_element_type=F32)
        p_d = jnp.exp(s_d - m)
        l_d = jnp.sum(p_d, axis=-1, keepdims=True)
        acc_d = jnp.dot(p_d.astype(BF16), v_ref[r0:r0 + ATT_TQ, :], preferred_element_type=F32)
        if qi > 0:
            l_d = l_d + l
            acc_d = acc_d + acc
        o_ref[r0:r0 + ATT_TQ, :] = acc_d / l_d


def _attention(q, k, v, batch, seq):
    qc = lax.broadcasted_iota(jnp.int32, (ATT_TQ, ATT_TQ), 0) // CHUNK
    kc = lax.broadcasted_iota(jnp.int32, (ATT_TQ, ATT_TQ), 1) // CHUNK
    bias = jnp.where(kc <= qc, 0.0, -jnp.inf).astype(F32)
    return pl.pallas_call(
        _attn_kernel,
        grid=(batch, N_HEADS),
        in_specs=[
            pl.BlockSpec((None, seq, HEAD_PAD), lambda b, h: (h, b, 0)),
            pl.BlockSpec((None, seq, HEAD_PAD), lambda b, h: (h, b, 0)),
            pl.BlockSpec((None, seq, V_HEAD), lambda b, h: (h, b, 0)),
            _const_spec(bias.shape),
        ],
        out_specs=pl.BlockSpec((seq, V_HEAD), lambda b, h: (b, h)),
        out_shape=jax.ShapeDtypeStruct((batch * seq, ATTN_WIDTH), F32),
        compiler_params=pltpu.CompilerParams(
            dimension_semantics=("parallel", "parallel"), vmem_limit_bytes=VMEM_LIMIT),
        name="attention",
    )(q, k, v, bias)


OUT_TM = 512


def _outproj_kernel(x_ref, ys_ref, ya_ref, an_ref, w_ref, o_ref):
    ya = _rms(ya_ref[...], an_ref[...]).astype(BF16)
    o_ref[...] = (x_ref[...]
                  + jnp.dot(ys_ref[...], w_ref[:SSM_WIDTH, :], preferred_element_type=F32)
                  + jnp.dot(ya, w_ref[SSM_WIDTH:, :], preferred_element_type=F32))


def _outproj(x, ys, ya, an, w):
    t, d = x.shape
    tm = OUT_TM
    row = lambda i: (i, 0)
    return pl.pallas_call(
        _outproj_kernel,
        grid=(t // tm,),
        in_specs=[
            pl.BlockSpec((tm, d), row),
            pl.BlockSpec((tm, SSM_WIDTH), row),
            pl.BlockSpec((tm, ATTN_WIDTH), row),
            _const_spec(an.shape), _const_spec(w.shape),
        ],
        out_specs=pl.BlockSpec((tm, d), row),
        out_shape=jax.ShapeDtypeStruct((t, d), F32),
        compiler_params=pltpu.CompilerParams(
            dimension_semantics=("parallel",), vmem_limit_bytes=VMEM_LIMIT),
        name="out_proj",
    )(x, ys, ya, an, w)


def _s5_params(log_step, a_re, a_im, b_re, b_im, c_re, c_im):
    g16 = MXU_DIM // SSM_GROUP
    dt = jnp.exp(log_step)[:, None]
    mag = jnp.exp(a_re * dt)
    ang = a_im * dt
    lr, li = mag * jnp.cos(ang), mag * jnp.sin(ang)
    den = a_re * a_re + a_im * a_im
    fr = ((lr - 1.0) * a_re + li * a_im) / den
    fi = (li * a_re - (lr - 1.0) * a_im) / den
    bb_r = fr[..., None] * b_re - fi[..., None] * b_im
    bb_i = fr[..., None] * b_im + fi[..., None] * b_re
    eye = jnp.eye(g16, dtype=F32)

    def bd_in(b):
        b = b.reshape(S5_KT, g16, SSM_STATE, SSM_GROUP)
        return jnp.einsum('jgnc,gh->jgchn', b, eye).reshape(S5_KT, MXU_DIM, S5_SW)

    def bd_out(c):
        c = c.reshape(S5_KT, g16, SSM_GROUP, SSM_STATE)
        return jnp.einsum('jgcn,gh->jgnhc', c, eye).reshape(S5_KT, S5_SW, MXU_DIM)

    wb = jnp.concatenate([bd_in(bb_r), bd_in(bb_i)], axis=2).astype(BF16)
    wc = jnp.concatenate([bd_out(c_re), bd_out(-c_im)], axis=1).astype(BF16)
    lam_r = lr.reshape(S5_KT, SUBLANES, LANES)
    lam_i = li.reshape(S5_KT, SUBLANES, LANES)
    return wb, lam_r, lam_i, wc


def _dup_rope(w):
    k = w.shape[0]
    w = w.reshape(k, -1, QK_HEAD)
    return jnp.concatenate([w, w[:, :, QK_NOPE:]], axis=2).reshape(k, -1)


def kernel(x, positions, ffn1_norm, ffn1_w_gate, ffn1_w_up, ffn1_w_down, mix_norm, w_in, ssm_log_step, ssm_a_re, ssm_a_im, ssm_b_re, ssm_b_im, ssm_c_re, ssm_c_im, ssm_d, ssm_w_glu, ssm_b_glu, mla_q_a_norm, mla_w_q_up, mla_kv_a_norm, mla_w_kv_up, mla_q_norm, mla_k_norm, ssm_out_norm, attn_out_norm, w_out, ffn2_norm, ffn2_w_gate, ffn2_w_up, ffn2_w_down, final_norm):
    batch, seq, d = x.shape
    depth = w_in.shape[0]
    t = batch * seq
    xt = x.reshape(t, d)

    inv_freq = ROPE_THETA ** (-jnp.arange(0, QK_ROPE, 2, dtype=F32) / QK_ROPE)
    half = QK_ROPE // 2
    lane = jnp.arange(LANES)
    ang = positions.astype(F32).reshape(t, 1) * jnp.tile(inv_freq, LANES // half)
    cos_t = jnp.where(lane < QK_ROPE, jnp.cos(ang), 0.0)
    sin_t = jnp.where(lane < half, -jnp.sin(ang), jnp.where(lane < QK_ROPE, jnp.sin(ang), 0.0))

    o3 = SSM_WIDTH + Q_LORA + KV_LORA
    row = lambda v: v.reshape(1, -1)
    for l in range(depth):
        xt = _ffn(xt, row(ffn1_norm[l]), ffn1_w_gate[l], ffn1_w_up[l], ffn1_w_down[l], row(final_norm[l]),
                  final_norm=False)

        w = jnp.concatenate([w_in[l], w_in[l][:, o3:]], axis=1).astype(BF16)
        qn = _dup_rope(row(mla_q_norm[l]))
        kn = _dup_rope(row(mla_k_norm[l]))
        u, q, k, v = _inproj(
            xt, row(mix_norm[l]), w, row(mla_q_a_norm[l]), _dup_rope(mla_w_q_up[l]).astype(BF16),
            row(mla_kv_a_norm[l]), mla_w_kv_up[l].astype(BF16), qn, kn, cos_t, sin_t)

        wb, lam_r, lam_i, wc = _s5_params(ssm_log_step[l], ssm_a_re[l], ssm_a_im[l], ssm_b_re[l],
                                          ssm_b_im[l], ssm_c_re[l], ssm_c_im[l])
        ys = _s5(u, wb, lam_r, lam_i, wc, row(ssm_d[l]), ssm_w_glu[l].astype(BF16), row(ssm_b_glu[l]),
                 row(ssm_out_norm[l]), batch, seq)

        ya = _attention(q, k, v, batch, seq)

        xt = _outproj(xt, ys, ya, row(attn_out_norm[l]), w_out[l].astype(BF16))

        xt = _ffn(xt, row(ffn2_norm[l]), ffn2_w_gate[l], ffn2_w_up[l], ffn2_w_down[l], row(final_norm[l]),
                  final_norm=True)
    return xt.reshape(batch, seq, d)
```

```python
import functools
import math

import jax
import jax.numpy as jnp
from jax import lax
from jax.experimental import pallas as pl
from jax.experimental.pallas import tpu as pltpu

D_MODEL = 2048
CHUNK = 64
SSM_WIDTH = 1024
SSM_GROUP = 16
SSM_GROUPS = 64
SSM_STATE = 64
N_HEADS = 8
QK_NOPE = 128
QK_ROPE = 64
QK_HEAD = QK_NOPE + QK_ROPE
V_HEAD = 128
ATTN_WIDTH = N_HEADS * V_HEAD
Q_LORA = 512
KV_LORA = 256
D_FF = 5632
ROPE_THETA = 10000.0
EPS = 1e-6

LANES = 128
SUBLANES = 8
MXU_DIM = 256
HEAD_PAD = 2 * LANES
VMEM_LIMIT = 60 * 1024 * 1024

BF16 = jnp.bfloat16
F32 = jnp.float32


def _rms(xf, g):
    return xf * lax.rsqrt(jnp.mean(xf * xf, axis=-1, keepdims=True) + EPS) * g


def _const_spec(shape):
    nd = len(shape)
    return pl.BlockSpec(shape, lambda *_: (0,) * nd, pipeline_mode=pl.Buffered(1))


FFN_TM = 1024
FFN_TF = 256


def _ffn_kernel(x_ref, g_ref, wg_ref, wu_ref, wd_ref, fg_ref, o_ref, h_ref, *, final_norm):
    j = pl.program_id(1)

    @pl.when(j == 0)
    def _():
        h_ref[...] = _rms(x_ref[...], g_ref[...]).astype(BF16)
        o_ref[...] = jnp.zeros_like(o_ref)

    h = h_ref[...]
    gate = jnp.dot(h, wg_ref[...].astype(BF16), preferred_element_type=F32)
    up = jnp.dot(h, wu_ref[...].astype(BF16), preferred_element_type=F32)
    act = (gate * jax.nn.sigmoid(gate) * up).astype(BF16)
    o_ref[...] += jnp.dot(act, wd_ref[...].astype(BF16), preferred_element_type=F32)

    @pl.when(j == pl.num_programs(1) - 1)
    def _():
        y = x_ref[...] + 0.5 * o_ref[...]
        if final_norm:
            y = _rms(y, fg_ref[...])
        o_ref[...] = y


def _ffn(x, g, wg, wu, wd, fg, final_norm):
    t, d = x.shape
    dff = wg.shape[1]
    grid = (t // FFN_TM, dff // FFN_TF)
    return pl.pallas_call(
        functools.partial(_ffn_kernel, final_norm=final_norm),
        grid=grid,
        in_specs=[
            pl.BlockSpec((FFN_TM, d), lambda i, j: (i, 0)),
            pl.BlockSpec((1, d), lambda i, j: (0, 0)),
            pl.BlockSpec((d, FFN_TF), lambda i, j: (0, j)),
            pl.BlockSpec((d, FFN_TF), lambda i, j: (0, j)),
            pl.BlockSpec((FFN_TF, d), lambda i, j: (j, 0)),
            pl.BlockSpec((1, d), lambda i, j: (0, 0)),
        ],
        out_specs=pl.BlockSpec((FFN_TM, d), lambda i, j: (i, 0)),
        out_shape=jax.ShapeDtypeStruct((t, d), F32),
        scratch_shapes=[pltpu.VMEM((FFN_TM, d), BF16)],
        compiler_params=pltpu.CompilerParams(
            dimension_semantics=("parallel", "arbitrary"), vmem_limit_bytes=VMEM_LIMIT),
        name="ffn_final" if final_norm else "ffn",
    )(x, g, wg, wu, wd, fg)


INP_TM = 512
INP_SUB = 2


def _rope_dup(r, cos_t, sin_t):
    return r * cos_t + pltpu.roll(r, LANES - QK_ROPE // 2, axis=1) * sin_t


def _inproj_kernel(x_ref, g_ref, w_ref, qan_ref, wqup_ref, kvan_ref, wkvup_ref, qn_ref, kn_ref,
                   cos_ref, sin_ref, u_ref, q_ref, k_ref, v_ref):
    sub = INP_TM // INP_SUB
    inv_head = 1.0 / QK_HEAD
    for sb in range(INP_SUB):
        rows = slice(sb * sub, (sb + 1) * sub)
        h = _rms(x_ref[rows, :], g_ref[...]).astype(BF16)
        u_ref[rows, :] = jnp.dot(h, w_ref[:, :SSM_WIDTH], preferred_element_type=F32)
        lat = jnp.dot(h, w_ref[:, SSM_WIDTH:], preferred_element_type=F32)
        ql = lat[:, :Q_LORA]
        kvl = lat[:, Q_LORA:Q_LORA + KV_LORA]
        kpe = lat[:, Q_LORA + KV_LORA:]
        q = jnp.dot(_rms(ql, qan_ref[...]).astype(BF16), wqup_ref[...], preferred_element_type=F32)
        kv = jnp.dot(_rms(kvl, kvan_ref[...]).astype(BF16), wkvup_ref[...], preferred_element_type=F32)
        cos_t = cos_ref[rows, :]
        sin_t = sin_ref[rows, :]
        qn = qn_ref[...]
        kn = kn_ref[...]
        kpe_rot = _rope_dup(kpe * kn[:, LANES:], cos_t, sin_t)
        kpe_ssq = 0.5 * jnp.sum(kpe * kpe, axis=-1, keepdims=True)
        for hd in range(N_HEADS):
            qno = q[:, hd * HEAD_PAD:hd * HEAD_PAD + QK_NOPE]
            qro = q[:, hd * HEAD_PAD + QK_NOPE:(hd + 1) * HEAD_PAD]
            r = lax.rsqrt(jnp.sum(qno * qno + 0.5 * (qro * qro), axis=-1, keepdims=True) * inv_head + EPS)
            q_ref[hd, rows, :LANES] = (qno * r * qn[:, :LANES]).astype(BF16)
            q_ref[hd, rows, LANES:] = _rope_dup(qro * r * qn[:, LANES:], cos_t, sin_t).astype(BF16)
            kno = kv[:, hd * HEAD_PAD:hd * HEAD_PAD + QK_NOPE]
            r = lax.rsqrt((jnp.sum(kno * kno, axis=-1, keepdims=True) + kpe_ssq) * inv_head + EPS)
            k_ref[hd, rows, :LANES] = (kno * r * kn[:, :LANES]).astype(BF16)
            k_ref[hd, rows, LANES:] = (kpe_rot * r).astype(BF16)
            v_ref[hd, rows, :] = kv[:, hd * HEAD_PAD + QK_NOPE:(hd + 1) * HEAD_PAD].astype(BF16)


def _inproj(x, g, w, qan, wqup, kvan, wkvup, qn, kn, cos_t, sin_t):
    t, d = x.shape
    tm = INP_TM
    row = lambda i: (i, 0)
    head_row = lambda i: (0, i, 0)
    return pl.pallas_call(
        _inproj_kernel,
        grid=(t // tm,),
        in_specs=[
            pl.BlockSpec((tm, d), row),
            _const_spec(g.shape), _const_spec(w.shape), _const_spec(qan.shape), _const_spec(wqup.shape),
            _const_spec(kvan.shape), _const_spec(wkvup.shape), _const_spec(qn.shape), _const_spec(kn.shape),
            pl.BlockSpec((tm, LANES), row),
            pl.BlockSpec((tm, LANES), row),
        ],
        out_specs=[
            pl.BlockSpec((tm, SSM_WIDTH), row),
            pl.BlockSpec((N_HEADS, tm, HEAD_PAD), head_row),
            pl.BlockSpec((N_HEADS, tm, HEAD_PAD), head_row),
            pl.BlockSpec((N_HEADS, tm, V_HEAD), head_row),
        ],
        out_shape=[
            jax.ShapeDtypeStruct((t, SSM_WIDTH), F32),
            jax.ShapeDtypeStruct((N_HEADS, t, HEAD_PAD), BF16),
            jax.ShapeDtypeStruct((N_HEADS, t, HEAD_PAD), BF16),
            jax.ShapeDtypeStruct((N_HEADS, t, V_HEAD), BF16),
        ],
        compiler_params=pltpu.CompilerParams(
            dimension_semantics=("parallel",), vmem_limit_bytes=VMEM_LIMIT),
        name="in_proj",
    )(x, g, w, qan, wqup, kvan, wkvup, qn, kn, cos_t, sin_t)


S5_TC = 256
S5_KT = SSM_WIDTH // MXU_DIM
S5_SW = MXU_DIM // SSM_GROUP * SSM_STATE
S5_TILES = 2 * S5_SW // LANES
S5_ROWS = S5_KT * S5_TILES * SUBLANES


def _s5_kernel(u_ref, wb_ref, lr_ref, li_ref, wc_ref, d_ref, wglu_ref, bglu_ref, on_ref,
               y_ref, bu_ref, st_ref, carry_ref):
    rg = S5_TC // SUBLANES

    @pl.when(pl.program_id(1) == 0)
    def _():
        carry_ref[...] = jnp.zeros_like(carry_ref)

    u = u_ref[...]
    ub = u.astype(BF16)
    for j in range(S5_KT):
        res = jnp.dot(ub[:, j * MXU_DIM:(j + 1) * MXU_DIM], wb_ref[j], preferred_element_type=F32)
        for c in range(S5_TILES):
            row0 = (j * S5_TILES + c) * SUBLANES
            bu_ref[:, row0:row0 + SUBLANES, :] = res[:, c * LANES:(c + 1) * LANES].reshape(rg, SUBLANES, LANES)

    lam = [(lr_ref[j], li_ref[j]) for j in range(S5_KT)]
    half = S5_TILES // 2 * SUBLANES

    def step(r, carry):
        for s in range(SUBLANES):
            nxt = []
            for j in range(S5_KT):
                base = j * S5_TILES * SUBLANES + s
                idx_r = pl.ds(base, SUBLANES, stride=SUBLANES)
                idx_i = pl.ds(base + half, SUBLANES, stride=SUBLANES)
                lr, li = lam[j]
                sr, si = carry[2 * j], carry[2 * j + 1]
                nr = lr * sr - li * si + bu_ref[r, idx_r, :]
                ni = lr * si + li * sr + bu_ref[r, idx_i, :]
                st_ref[r, idx_r, :] = nr
                st_ref[r, idx_i, :] = ni
                nxt += [nr, ni]
            carry = tuple(nxt)
        return carry

    init = tuple(carry_ref[v] for v in range(2 * S5_KT))
    fin = lax.fori_loop(0, rg, step, init)
    for v in range(2 * S5_KT):
        carry_ref[v] = fin[v]

    ys = []
    for j in range(S5_KT):
        tiles = [st_ref[:, (j * S5_TILES + c) * SUBLANES:(j * S5_TILES + c + 1) * SUBLANES, :]
                 .reshape(S5_TC, LANES) for c in range(S5_TILES)]
        sj = jnp.concatenate(tiles, axis=1).astype(BF16)
        ys.append(jnp.dot(sj, wc_ref[j], preferred_element_type=F32))
    y = jnp.concatenate(ys, axis=1) + d_ref[...] * u
    y = jax.nn.gelu(y)
    y = y * jax.nn.sigmoid(jnp.dot(y.astype(BF16), wglu_ref[...], preferred_element_type=F32) + bglu_ref[...])
    y_ref[...] = _rms(y, on_ref[...]).astype(BF16)


def _s5(u, wb, lr, li, wc, d, wglu, bglu, on, batch, seq):
    nchunk = seq // S5_TC
    row = lambda b, c: (b * nchunk + c, 0)
    rg = S5_TC // SUBLANES
    return pl.pallas_call(
        _s5_kernel,
        grid=(batch, nchunk),
        in_specs=[
            pl.BlockSpec((S5_TC, SSM_WIDTH), row),
            _const_spec(wb.shape), _const_spec(lr.shape), _const_spec(li.shape), _const_spec(wc.shape),
            _const_spec(d.shape), _const_spec(wglu.shape), _const_spec(bglu.shape), _const_spec(on.shape),
        ],
        out_specs=pl.BlockSpec((S5_TC, SSM_WIDTH), row),
        out_shape=jax.ShapeDtypeStruct((batch * seq, SSM_WIDTH), BF16),
        scratch_shapes=[
            pltpu.VMEM((rg, S5_ROWS, LANES), F32),
            pltpu.VMEM((rg, S5_ROWS, LANES), F32),
            pltpu.VMEM((2 * S5_KT, SUBLANES, LANES), F32),
        ],
        compiler_params=pltpu.CompilerParams(
            dimension_semantics=("parallel", "arbitrary"), vmem_limit_bytes=VMEM_LIMIT),
        name="s5",
    )(u, wb, lr, li, wc, d, wglu, bglu, on)


ATT_TQ = 256


def _attn_kernel(q_ref, k_ref, v_ref, bias_ref, o_ref):
    seq = q_ref.shape[0]
    scale = QK_HEAD ** -0.5
    nt = (((1,), (1,)), ((), ()))
    bias = bias_ref[...]
    for qi in range(seq // ATT_TQ):
        r0 = qi * ATT_TQ
        q = q_ref[r0:r0 + ATT_TQ, :]
        s_d = lax.dot_general(q, k_ref[r0:r0 + ATT_TQ, :], nt, preferred_element_type=F32) * scale + bias
        m = jnp.max(s_d, axis=-1, keepdims=True)
        if qi > 0:
            s_o = lax.dot_general(q, k_ref[:r0, :], nt, preferred_element_type=F32) * scale
            m = jnp.maximum(m, jnp.max(s_o, axis=-1, keepdims=True))
            p_o = jnp.exp(s_o - m)
            l = jnp.sum(p_o, axis=-1, keepdims=True)
            acc = jnp.dot(p_o.astype(BF16), v_ref[:r0, :], preferred_element_type=F32)
        p_d = jnp.exp(s_d - m)
        l_d = jnp.sum(p_d, axis=-1, keepdims=True)
        acc_d = jnp.dot(p_d.astype(BF16), v_ref[r0:r0 + ATT_TQ, :], preferred_element_type=F32)
        if qi > 0:
            l_d = l_d + l
            acc_d = acc_d + acc
        o_ref[r0:r0 + ATT_TQ, :] = acc_d / l_d


def _attention(q, k, v, batch, seq):
    qc = lax.broadcasted_iota(jnp.int32, (ATT_TQ, ATT_TQ), 0) // CHUNK
    kc = lax.broadcasted_iota(jnp.int32, (ATT_TQ, ATT_TQ), 1) // CHUNK
    bias = jnp.where(kc <= qc, 0.0, -jnp.inf).astype(F32)
    return pl.pallas_call(
        _attn_kernel,
        grid=(batch, N_HEADS),
        in_specs=[
            pl.BlockSpec((None, seq, HEAD_PAD), lambda b, h: (h, b, 0)),
            pl.BlockSpec((None, seq, HEAD_PAD), lambda b, h: (h, b, 0)),
            pl.BlockSpec((None, seq, V_HEAD), lambda b, h: (h, b, 0)),
            _const_spec(bias.shape),
        ],
        out_specs=pl.BlockSpec((seq, V_HEAD), lambda b, h: (b, h)),
        out_shape=jax.ShapeDtypeStruct((batch * seq, ATTN_WIDTH), F32),
        compiler_params=pltpu.CompilerParams(
            dimension_semantics=("parallel", "parallel"), vmem_limit_bytes=VMEM_LIMIT),
        name="attention",
    )(q, k, v, bias)


OUT_TM = 512


def _outproj_kernel(x_ref, ys_ref, ya_ref, an_ref, w_ref, o_ref):
    ya = _rms(ya_ref[...], an_ref[...]).astype(BF16)
    o_ref[...] = (x_ref[...]
                  + jnp.dot(ys_ref[...], w_ref[:SSM_WIDTH, :], preferred_element_type=F32)
                  + jnp.dot(ya, w_ref[SSM_WIDTH:, :], preferred_element_type=F32))


def _outproj(x, ys, ya, an, w):
    t, d = x.shape
    tm = OUT_TM
    row = lambda i: (i, 0)
    return pl.pallas_call(
        _outproj_kernel,
        grid=(t // tm,),
        in_specs=[
            pl.BlockSpec((tm, d), row),
            pl.BlockSpec((tm, SSM_WIDTH), row),
            pl.BlockSpec((tm, ATTN_WIDTH), row),
            _const_spec(an.shape), _const_spec(w.shape),
        ],
        out_specs=pl.BlockSpec((tm, d), row),
        out_shape=jax.ShapeDtypeStruct((t, d), F32),
        compiler_params=pltpu.CompilerParams(
            dimension_semantics=("parallel",), vmem_limit_bytes=VMEM_LIMIT),
        name="out_proj",
    )(x, ys, ya, an, w)


def _s5_params(log_step, a_re, a_im, b_re, b_im, c_re, c_im):
    g16 = MXU_DIM // SSM_GROUP
    dt = jnp.exp(log_step)[:, None]
    mag = jnp.exp(a_re * dt)
    ang = a_im * dt
    lr, li = mag * jnp.cos(ang), mag * jnp.sin(ang)
    den = a_re * a_re + a_im * a_im
    fr = ((lr - 1.0) * a_re + li * a_im) / den
    fi = (li * a_re - (lr - 1.0) * a_im) / den
    bb_r = fr[..., None] * b_re - fi[..., None] * b_im
    bb_i = fr[..., None] * b_im + fi[..., None] * b_re
    eye = jnp.eye(g16, dtype=F32)

    def bd_in(b):
        b = b.reshape(S5_KT, g16, SSM_STATE, SSM_GROUP)
        return jnp.einsum('jgnc,gh->jgchn', b, eye).reshape(S5_KT, MXU_DIM, S5_SW)

    def bd_out(c):
        c = c.reshape(S5_KT, g16, SSM_GROUP, SSM_STATE)
        return jnp.einsum('jgcn,gh->jgnhc', c, eye).reshape(S5_KT, S5_SW, MXU_DIM)

    wb = jnp.concatenate([bd_in(bb_r), bd_in(bb_i)], axis=2).astype(BF16)
    wc = jnp.concatenate([bd_out(c_re), bd_out(-c_im)], axis=1).astype(BF16)
    lam_r = lr.reshape(S5_KT, SUBLANES, LANES)
    lam_i = li.reshape(S5_KT, SUBLANES, LANES)
    return wb, lam_r, lam_i, wc


def _dup_rope(w):
    k = w.shape[0]
    w = w.reshape(k, -1, QK_HEAD)
    return jnp.concatenate([w, w[:, :, QK_NOPE:]], axis=2).reshape(k, -1)


def kernel(x, positions, ffn1_norm, ffn1_w_gate, ffn1_w_up, ffn1_w_down, mix_norm, w_in, ssm_log_step, ssm_a_re, ssm_a_im, ssm_b_re, ssm_b_im, ssm_c_re, ssm_c_im, ssm_d, ssm_w_glu, ssm_b_glu, mla_q_a_norm, mla_w_q_up, mla_kv_a_norm, mla_w_kv_up, mla_q_norm, mla_k_norm, ssm_out_norm, attn_out_norm, w_out, ffn2_norm, ffn2_w_gate, ffn2_w_up, ffn2_w_down, final_norm):
    batch, seq, d = x.shape
    depth = w_in.shape[0]
    t = batch * seq
    xt = x.reshape(t, d)

    inv_freq = ROPE_THETA ** (-jnp.arange(0, QK_ROPE, 2, dtype=F32) / QK_ROPE)
    half = QK_ROPE // 2
    lane = jnp.arange(LANES)
    ang = positions.astype(F32).reshape(t, 1) * jnp.tile(inv_freq, LANES // half)
    cos_t = jnp.where(lane < QK_ROPE, jnp.cos(ang), 0.0)
    sin_t = jnp.where(lane < half, -jnp.sin(ang), jnp.where(lane < QK_ROPE, jnp.sin(ang), 0.0))

    o3 = SSM_WIDTH + Q_LORA + KV_LORA
    row = lambda v: v.reshape(1, -1)
    for l in range(depth):
        xt = _ffn(xt, row(ffn1_norm[l]), ffn1_w_gate[l], ffn1_w_up[l], ffn1_w_down[l], row(final_norm[l]),
                  final_norm=False)

        w = jnp.concatenate([w_in[l], w_in[l][:, o3:]], axis=1).astype(BF16)
        qn = _dup_rope(row(mla_q_norm[l]))
        kn = _dup_rope(row(mla_k_norm[l]))
        u, q, k, v = _inproj(
            xt, row(mix_norm[l]), w, row(mla_q_a_norm[l]), _dup_rope(mla_w_q_up[l]).astype(BF16),
            row(mla_kv_a_norm[l]), mla_w_kv_up[l].astype(BF16), qn, kn, cos_t, sin_t)

        wb, lam_r, lam_i, wc = _s5_params(ssm_log_step[l], ssm_a_re[l], ssm_a_im[l], ssm_b_re[l],
                                          ssm_b_im[l], ssm_c_re[l], ssm_c_im[l])
        ys = _s5(u, wb, lam_r, lam_i, wc, row(ssm_d[l]), ssm_w_glu[l].astype(BF16), row(ssm_b_glu[l]),
                 row(ssm_out_norm[l]), batch, seq)

        ya = _attention(q, k, v, batch, seq)

        xt = _outproj(xt, ys, ya, row(attn_out_norm[l]), w_out[l].astype(BF16))

        xt = _ffn(xt, row(ffn2_norm[l]), ffn2_w_gate[l], ffn2_w_up[l], ffn2_w_down[l], row(final_norm[l]),
                  final_norm=True)
    return xt.reshape(batch, seq, d)
```

```python
import functools
import math

import jax
import jax.numpy as jnp
from jax import lax
from jax.experimental import pallas as pl
from jax.experimental.pallas import tpu as pltpu

D_MODEL = 2048
CHUNK = 64
SSM_WIDTH = 1024
SSM_GROUP = 16
SSM_GROUPS = 64
SSM_STATE = 64
N_HEADS = 8
QK_NOPE = 128
QK_ROPE = 64
QK_HEAD = QK_NOPE + QK_ROPE
V_HEAD = 128
ATTN_WIDTH = N_HEADS * V_HEAD
Q_LORA = 512
KV_LORA = 256
D_FF = 5632
ROPE_THETA = 10000.0
EPS = 1e-6

LANES = 128
SUBLANES = 8
MXU_DIM = 256
HEAD_PAD = 2 * LANES
VMEM_LIMIT = 60 * 1024 * 1024

BF16 = jnp.bfloat16
F32 = jnp.float32


def _rms(xf, g):
    return xf * lax.rsqrt(jnp.mean(xf * xf, axis=-1, keepdims=True) + EPS) * g


def _const_spec(shape):
    nd = len(shape)
    return pl.BlockSpec(shape, lambda *_: (0,) * nd, pipeline_mode=pl.Buffered(1))


FFN_TM = 1024
FFN_TF = 256


def _ffn_kernel(x_ref, g_ref, wg_ref, wu_ref, wd_ref, fg_ref, o_ref, h_ref, *, final_norm):
    j = pl.program_id(1)

    @pl.when(j == 0)
    def _():
        h_ref[...] = _rms(x_ref[...], g_ref[...]).astype(BF16)
        o_ref[...] = jnp.zeros_like(o_ref)

    h = h_ref[...]
    gate = jnp.dot(h, wg_ref[...].astype(BF16), preferred_element_type=F32)
    up = jnp.dot(h, wu_ref[...].astype(BF16), preferred_element_type=F32)
    act = (gate * jax.nn.sigmoid(gate) * up).astype(BF16)
    o_ref[...] += jnp.dot(act, wd_ref[...].astype(BF16), preferred_element_type=F32)

    @pl.when(j == pl.num_programs(1) - 1)
    def _():
        y = x_ref[...] + 0.5 * o_ref[...]
        if final_norm:
            y = _rms(y, fg_ref[...])
        o_ref[...] = y


def _ffn(x, g, wg, wu, wd, fg, final_norm):
    t, d = x.shape
    dff = wg.shape[1]
    grid = (t // FFN_TM, dff // FFN_TF)
    return pl.pallas_call(
        functools.partial(_ffn_kernel, final_norm=final_norm),
        grid=grid,
        in_specs=[
            pl.BlockSpec((FFN_TM, d), lambda i, j: (i, 0)),
            pl.BlockSpec((1, d), lambda i, j: (0, 0)),
            pl.BlockSpec((d, FFN_TF), lambda i, j: (0, j)),
            pl.BlockSpec((d, FFN_TF), lambda i, j: (0, j)),
            pl.BlockSpec((FFN_TF, d), lambda i, j: (j, 0)),
            pl.BlockSpec((1, d), lambda i, j: (0, 0)),
        ],
        out_specs=pl.BlockSpec((FFN_TM, d), lambda i, j: (i, 0)),
        out_shape=jax.ShapeDtypeStruct((t, d), F32),
        scratch_shapes=[pltpu.VMEM((FFN_TM, d), BF16)],
        compiler_params=pltpu.CompilerParams(
            dimension_semantics=("parallel", "arbitrary"), vmem_limit_bytes=VMEM_LIMIT),
        name="ffn_final" if final_norm else "ffn",
    )(x, g, wg, wu, wd, fg)


INP_TM = 512
INP_SUB = 2


def _rope_dup(r, cos_t, sin_t):
    return r * cos_t + pltpu.roll(r, LANES - QK_ROPE // 2, axis=1) * sin_t


def _inproj_kernel(x_ref, g_ref, w_ref, qan_ref, wqup_ref, kvan_ref, wkvup_ref, qn_ref, kn_ref,
                   cos_ref, sin_ref, u_ref, q_ref, k_ref, v_ref):
    sub = INP_TM // INP_SUB
    inv_head = 1.0 / QK_HEAD
    for sb in range(INP_SUB):
        rows = slice(sb * sub, (sb + 1) * sub)
        h = _rms(x_ref[rows, :], g_ref[...]).astype(BF16)
        u_ref[rows, :] = jnp.dot(h, w_ref[:, :SSM_WIDTH], preferred_element_type=F32)
        lat = jnp.dot(h, w_ref[:, SSM_WIDTH:], preferred_element_type=F32)
        ql = lat[:, :Q_LORA]
        kvl = lat[:, Q_LORA:Q_LORA + KV_LORA]
        kpe = lat[:, Q_LORA + KV_LORA:]
        q = jnp.dot(_rms(ql, qan_ref[...]).astype(BF16), wqup_ref[...], preferred_element_type=F32)
        kv = jnp.dot(_rms(kvl, kvan_ref[...]).astype(BF16), wkvup_ref[...], preferred_element_type=F32)
        cos_t = cos_ref[rows, :]
        sin_t = sin_ref[rows, :]
        qn = qn_ref[...]
        kn = kn_ref[...]
        kpe_rot = _rope_dup(kpe * kn[:, LANES:], cos_t, sin_t)
        kpe_ssq = 0.5 * jnp.sum(kpe * kpe, axis=-1, keepdims=True)
        for hd in range(N_HEADS):
            qno = q[:, hd * HEAD_PAD:hd * HEAD_PAD + QK_NOPE]
            qro = q[:, hd * HEAD_PAD + QK_NOPE:(hd + 1) * HEAD_PAD]
            r = lax.rsqrt(jnp.sum(qno * qno + 0.5 * (qro * qro), axis=-1, keepdims=True) * inv_head + EPS)
            q_ref[hd, rows, :LANES] = (qno * r * qn[:, :LANES]).astype(BF16)
            q_ref[hd, rows, LANES:] = _rope_dup(qro * r * qn[:, LANES:], cos_t, sin_t).astype(BF16)
            kno = kv[:, hd * HEAD_PAD:hd * HEAD_PAD + QK_NOPE]
            r = lax.rsqrt((jnp.sum(kno * kno, axis=-1, keepdims=True) + kpe_ssq) * inv_head + EPS)
            k_ref[hd, rows, :LANES] = (kno * r * kn[:, :LANES]).astype(BF16)
            k_ref[hd, rows, LANES:] = (kpe_rot * r).astype(BF16)
            v_ref[hd, rows, :] = kv[:, hd * HEAD_PAD + QK_NOPE:(hd + 1) * HEAD_PAD].astype(BF16)


def _inproj(x, g, w, qan, wqup, kvan, wkvup, qn, kn, cos_t, sin_t):
    t, d = x.shape
    tm = INP_TM
    row = lambda i: (i, 0)
    head_row = lambda i: (0, i, 0)
    return pl.pallas_call(
        _inproj_kernel,
        grid=(t // tm,),
        in_specs=[
            pl.BlockSpec((tm, d), row),
            _const_spec(g.shape), _const_spec(w.shape), _const_spec(qan.shape), _const_spec(wqup.shape),
            _const_spec(kvan.shape), _const_spec(wkvup.shape), _const_spec(qn.shape), _const_spec(kn.shape),
            pl.BlockSpec((tm, LANES), row),
            pl.BlockSpec((tm, LANES), row),
        ],
        out_specs=[
            pl.BlockSpec((tm, SSM_WIDTH), row),
            pl.BlockSpec((N_HEADS, tm, HEAD_PAD), head_row),
            pl.BlockSpec((N_HEADS, tm, HEAD_PAD), head_row),
            pl.BlockSpec((N_HEADS, tm, V_HEAD), head_row),
        ],
        out_shape=[
            jax.ShapeDtypeStruct((t, SSM_WIDTH), F32),
            jax.ShapeDtypeStruct((N_HEADS, t, HEAD_PAD), BF16),
            jax.ShapeDtypeStruct((N_HEADS, t, HEAD_PAD), BF16),
            jax.ShapeDtypeStruct((N_HEADS, t, V_HEAD), BF16),
        ],
        compiler_params=pltpu.CompilerParams(
            dimension_semantics=("parallel",), vmem_limit_bytes=VMEM_LIMIT),
        name="in_proj",
    )(x, g, w, qan, wqup, kvan, wkvup, qn, kn, cos_t, sin_t)


S5_TC = 256
S5_KT = SSM_WIDTH // MXU_DIM
S5_SW = MXU_DIM // SSM_GROUP * SSM_STATE
S5_TILES = 2 * S5_SW // LANES
S5_ROWS = S5_KT * S5_TILES * SUBLANES
S5_PITCH = (2 * S5_KT + 1) * SUBLANES


def _s5_kernel(u_ref, wb_ref, lr_ref, li_ref, wc_ref, d_ref, wglu_ref, bglu_ref, on_ref,
               y_ref, bu_ref, st_ref, carry_ref):
    rg = S5_TC // SUBLANES

    @pl.when(pl.program_id(1) == 0)
    def _():
        carry_ref[...] = jnp.zeros_like(carry_ref)

    u = u_ref[...]
    ub = u.astype(BF16)
    for j in range(S5_KT):
        res = jnp.dot(ub[:, j * MXU_DIM:(j + 1) * MXU_DIM], wb_ref[j], preferred_element_type=F32)
        for c in range(S5_TILES):
            row0 = (j * S5_TILES + c) * SUBLANES
            bu_ref[:, row0:row0 + SUBLANES, :] = res[:, c * LANES:(c + 1) * LANES].reshape(rg, SUBLANES, LANES)

    lam = [(lr_ref[j], li_ref[j]) for j in range(S5_KT)]
    half = S5_TILES // 2 * SUBLANES

    def step(r, carry):
        for s in range(SUBLANES):
            nxt = []
            for j in range(S5_KT):
                base = j * S5_TILES * SUBLANES + s
                idx_r = pl.ds(base, SUBLANES, stride=SUBLANES)
                idx_i = pl.ds(base + half, SUBLANES, stride=SUBLANES)
                lr, li = lam[j]
                sr, si = carry[2 * j], carry[2 * j + 1]
                nr = lr * sr - li * si + bu_ref[r, idx_r, :]
                ni = lr * si + li * sr + bu_ref[r, idx_i, :]
                row0 = pl.multiple_of(r * (SUBLANES * S5_PITCH), SUBLANES) + s * S5_PITCH
                st_ref[pl.ds(row0 + 2 * j * SUBLANES, SUBLANES), :] = nr
                st_ref[pl.ds(row0 + (2 * j + 1) * SUBLANES, SUBLANES), :] = ni
                nxt += [nr, ni]
            carry = tuple(nxt)
        return carry

    init = tuple(carry_ref[v] for v in range(2 * S5_KT))
    fin = lax.fori_loop(0, rg, step, init)
    for v in range(2 * S5_KT):
        carry_ref[v] = fin[v]

    ys = []
    for j in range(S5_KT):
        tiles = [st_ref[pl.ds(2 * j * SUBLANES + c, S5_TC, stride=S5_PITCH), :] for c in range(S5_TILES)]
        sj = jnp.concatenate(tiles, axis=1).astype(BF16)
        ys.append(jnp.dot(sj, wc_ref[j], preferred_element_type=F32))
    y = jnp.concatenate(ys, axis=1) + d_ref[...] * u
    y = jax.nn.gelu(y)
    y = y * jax.nn.sigmoid(jnp.dot(y.astype(BF16), wglu_ref[...], preferred_element_type=F32) + bglu_ref[...])
    y_ref[...] = _rms(y, on_ref[...]).astype(BF16)


def _s5(u, wb, lr, li, wc, d, wglu, bglu, on, batch, seq):
    nchunk = seq // S5_TC
    row = lambda b, c: (b * nchunk + c, 0)
    rg = S5_TC // SUBLANES
    return pl.pallas_call(
        _s5_kernel,
        grid=(batch, nchunk),
        in_specs=[
            pl.BlockSpec((S5_TC, SSM_WIDTH), row),
            _const_spec(wb.shape), _const_spec(lr.shape), _const_spec(li.shape), _const_spec(wc.shape),
            _const_spec(d.shape), _const_spec(wglu.shape), _const_spec(bglu.shape), _const_spec(on.shape),
        ],
        out_specs=pl.BlockSpec((S5_TC, SSM_WIDTH), row),
        out_shape=jax.ShapeDtypeStruct((batch * seq, SSM_WIDTH), BF16),
        scratch_shapes=[
            pltpu.VMEM((rg, S5_ROWS, LANES), F32),
            pltpu.VMEM((S5_TC * S5_PITCH, LANES), F32),
            pltpu.VMEM((2 * S5_KT, SUBLANES, LANES), F32),
        ],
        compiler_params=pltpu.CompilerParams(
            dimension_semantics=("parallel", "arbitrary"), vmem_limit_bytes=VMEM_LIMIT),
        name="s5",
    )(u, wb, lr, li, wc, d, wglu, bglu, on)


ATT_TQ = 512


def _attn_kernel(q_ref, k_ref, v_ref, bias_ref, o_ref):
    seq = q_ref.shape[0]
    scale = QK_HEAD ** -0.5 * math.log2(math.e)
    nt = (((1,), (1,)), ((), ()))
    bias = bias_ref[...]
    for qi in range(seq // ATT_TQ):
        r0 = qi * ATT_TQ
        q = q_ref[r0:r0 + ATT_TQ, :]
        s_d = lax.dot_general(q, k_ref[r0:r0 + ATT_TQ, :], nt, preferred_element_type=F32) * scale + bias
        m = jnp.max(s_d, axis=-1, keepdims=True)
        if qi > 0:
            s_o = lax.dot_general(q, k_ref[:r0, :], nt, preferred_element_type=F32) * scale
            m = jnp.maximum(m, jnp.max(s_o, axis=-1, keepdims=True))
            p_o = jnp.exp2(s_o - m)
            l = jnp.sum(p_o, axis=-1, keepdims=True)
            acc = jnp.dot(p_o.astype(BF16), v_ref[:r0, :], preferred_element_type=F32)
        p_d = jnp.exp2(s_d - m)
        l_d = jnp.sum(p_d, axis=-1, keepdims=True)
        acc_d = jnp.dot(p_d.astype(BF16), v_ref[r0:r0 + ATT_TQ, :], preferred_element_type=F32)
        if qi > 0:
            l_d = l_d + l
            acc_d = acc_d + acc
        o_ref[r0:r0 + ATT_TQ, :] = acc_d / l_d


def _attention(q, k, v, batch, seq):
    qc = lax.broadcasted_iota(jnp.int32, (ATT_TQ, ATT_TQ), 0) // CHUNK
    kc = lax.broadcasted_iota(jnp.int32, (ATT_TQ, ATT_TQ), 1) // CHUNK
    bias = jnp.where(kc <= qc, 0.0, -jnp.inf).astype(F32)
    return pl.pallas_call(
        _attn_kernel,
        grid=(batch, N_HEADS),
        in_specs=[
            pl.BlockSpec((None, seq, HEAD_PAD), lambda b, h: (h, b, 0)),
            pl.BlockSpec((None, seq, HEAD_PAD), lambda b, h: (h, b, 0)),
            pl.BlockSpec((None, seq, V_HEAD), lambda b, h: (h, b, 0)),
            _const_spec(bias.shape),
        ],
        out_specs=pl.BlockSpec((seq, V_HEAD), lambda b, h: (b, h)),
        out_shape=jax.ShapeDtypeStruct((batch * seq, ATTN_WIDTH), F32),
        compiler_params=pltpu.CompilerParams(
            dimension_semantics=("parallel", "parallel"), vmem_limit_bytes=VMEM_LIMIT),
        name="attention",
    )(q, k, v, bias)


OUT_TM = 512


def _outproj_kernel(x_ref, ys_ref, ya_ref, an_ref, w_ref, o_ref):
    ya = _rms(ya_ref[...], an_ref[...]).astype(BF16)
    o_ref[...] = (x_ref[...]
                  + jnp.dot(ys_ref[...], w_ref[:SSM_WIDTH, :], preferred_element_type=F32)
                  + jnp.dot(ya, w_ref[SSM_WIDTH:, :], preferred_element_type=F32))


def _outproj(x, ys, ya, an, w):
    t, d = x.shape
    tm = OUT_TM
    row = lambda i: (i, 0)
    return pl.pallas_call(
        _outproj_kernel,
        grid=(t // tm,),
        in_specs=[
            pl.BlockSpec((tm, d), row),
            pl.BlockSpec((tm, SSM_WIDTH), row),
            pl.BlockSpec((tm, ATTN_WIDTH), row),
            _const_spec(an.shape), _const_spec(w.shape),
        ],
        out_specs=pl.BlockSpec((tm, d), row),
        out_shape=jax.ShapeDtypeStruct((t, d), F32),
        compiler_params=pltpu.CompilerParams(
            dimension_semantics=("parallel",), vmem_limit_bytes=VMEM_LIMIT),
        name="out_proj",
    )(x, ys, ya, an, w)


def _s5_params(log_step, a_re, a_im, b_re, b_im, c_re, c_im):
    g16 = MXU_DIM // SSM_GROUP
    dt = jnp.exp(log_step)[:, None]
    mag = jnp.exp(a_re * dt)
    ang = a_im * dt
    lr, li = mag * jnp.cos(ang), mag * jnp.sin(ang)
    den = a_re * a_re + a_im * a_im
    fr = ((lr - 1.0) * a_re + li * a_im) / den
    fi = (li * a_re - (lr - 1.0) * a_im) / den
    bb_r = fr[..., None] * b_re - fi[..., None] * b_im
    bb_i = fr[..., None] * b_im + fi[..., None] * b_re
    ch_grp = jnp.arange(MXU_DIM) // SSM_GROUP
    st_grp = jnp.arange(S5_SW) // SSM_STATE

    def bd_in(b):
        rows = b.transpose(0, 2, 1).reshape(S5_KT, MXU_DIM, SSM_STATE)
        return jnp.where(ch_grp[:, None] == st_grp[None, :], jnp.tile(rows, (1, 1, g16)), 0.0)

    def bd_out(c):
        rows = c.transpose(0, 2, 1).reshape(S5_KT, S5_SW, SSM_GROUP)
        return jnp.where(st_grp[:, None] == ch_grp[None, :], jnp.tile(rows, (1, 1, g16)), 0.0)

    wb = jnp.concatenate([bd_in(bb_r), bd_in(bb_i)], axis=2).astype(BF16)
    wc = jnp.concatenate([bd_out(c_re), bd_out(-c_im)], axis=1).astype(BF16)
    lam_r = lr.reshape(S5_KT, SUBLANES, LANES)
    lam_i = li.reshape(S5_KT, SUBLANES, LANES)
    return wb, lam_r, lam_i, wc


def _dup_rope(w):
    k = w.shape[0]
    w = w.reshape(k, -1, QK_HEAD)
    return jnp.concatenate([w, w[:, :, QK_NOPE:]], axis=2).reshape(k, -1)


def kernel(x, positions, ffn1_norm, ffn1_w_gate, ffn1_w_up, ffn1_w_down, mix_norm, w_in, ssm_log_step, ssm_a_re, ssm_a_im, ssm_b_re, ssm_b_im, ssm_c_re, ssm_c_im, ssm_d, ssm_w_glu, ssm_b_glu, mla_q_a_norm, mla_w_q_up, mla_kv_a_norm, mla_w_kv_up, mla_q_norm, mla_k_norm, ssm_out_norm, attn_out_norm, w_out, ffn2_norm, ffn2_w_gate, ffn2_w_up, ffn2_w_down, final_norm):
    batch, seq, d = x.shape
    depth = w_in.shape[0]
    t = batch * seq
    xt = x.reshape(t, d)

    inv_freq = ROPE_THETA ** (-jnp.arange(0, QK_ROPE, 2, dtype=F32) / QK_ROPE)
    half = QK_ROPE // 2
    lane = jnp.arange(LANES)
    ang = positions.astype(F32).reshape(t, 1) * jnp.tile(inv_freq, LANES // half)
    cos_t = jnp.where(lane < QK_ROPE, jnp.cos(ang), 0.0)
    sin_t = jnp.where(lane < half, -jnp.sin(ang), jnp.where(lane < QK_ROPE, jnp.sin(ang), 0.0))

    o3 = SSM_WIDTH + Q_LORA + KV_LORA
    row = lambda v: v.reshape(1, -1)
    for l in range(depth):
        xt = _ffn(xt, row(ffn1_norm[l]), ffn1_w_gate[l], ffn1_w_up[l], ffn1_w_down[l], row(final_norm[l]),
                  final_norm=False)

        w = jnp.concatenate([w_in[l], w_in[l][:, o3:]], axis=1).astype(BF16)
        qn = _dup_rope(row(mla_q_norm[l]))
        kn = _dup_rope(row(mla_k_norm[l]))
        u, q, k, v = _inproj(
            xt, row(mix_norm[l]), w, row(mla_q_a_norm[l]), _dup_rope(mla_w_q_up[l]).astype(BF16),
            row(mla_kv_a_norm[l]), mla_w_kv_up[l].astype(BF16), qn, kn, cos_t, sin_t)

        wb, lam_r, lam_i, wc = _s5_params(ssm_log_step[l], ssm_a_re[l], ssm_a_im[l], ssm_b_re[l],
                                          ssm_b_im[l], ssm_c_re[l], ssm_c_im[l])
        ys = _s5(u, wb, lam_r, lam_i, wc, row(ssm_d[l]), ssm_w_glu[l].astype(BF16), row(ssm_b_glu[l]),
                 row(ssm_out_norm[l]), batch, seq)

        ya = _attention(q, k, v, batch, seq)

        xt = _outproj(xt, ys, ya, row(attn_out_norm[l]), w_out[l].astype(BF16))

        xt = _ffn(xt, row(ffn2_norm[l]), ffn2_w_gate[l], ffn2_w_up[l], ffn2_w_down[l], row(final_norm[l]),
                  final_norm=True)
    return xt.reshape(batch, seq, d)
```

```python
import functools
import math

import jax
import jax.numpy as jnp
from jax import lax
from jax.experimental import pallas as pl
from jax.experimental.pallas import tpu as pltpu

D_MODEL = 2048
CHUNK = 64
SSM_WIDTH = 1024
SSM_GROUP = 16
SSM_GROUPS = 64
SSM_STATE = 64
N_HEADS = 8
QK_NOPE = 128
QK_ROPE = 64
QK_HEAD = QK_NOPE + QK_ROPE
V_HEAD = 128
ATTN_WIDTH = N_HEADS * V_HEAD
Q_LORA = 512
KV_LORA = 256
D_FF = 5632
ROPE_THETA = 10000.0
EPS = 1e-6

LANES = 128
SUBLANES = 8
MXU_DIM = 256
HEAD_PAD = 2 * LANES
VMEM_LIMIT = 60 * 1024 * 1024

BF16 = jnp.bfloat16
F32 = jnp.float32


def _rms(xf, g):
    return xf * lax.rsqrt(jnp.mean(xf * xf, axis=-1, keepdims=True) + EPS) * g


def _const_spec(shape):
    nd = len(shape)
    return pl.BlockSpec(shape, lambda *_: (0,) * nd, pipeline_mode=pl.Buffered(1))


FFN_TM = 1024
FFN_TF = 256


def _ffn_kernel(x_ref, g_ref, wg_ref, wu_ref, wd_ref, fg_ref, o_ref, h_ref, *, final_norm):
    j = pl.program_id(1)

    @pl.when(j == 0)
    def _():
        x = x_ref[...]
        h_ref[...] = _rms(x, g_ref[...]).astype(BF16)
        o_ref[...] = x

    h = h_ref[...]
    gate = jnp.dot(h, wg_ref[...].astype(BF16), preferred_element_type=F32)
    up = jnp.dot(h, wu_ref[...].astype(BF16), preferred_element_type=F32)
    act = (gate * jax.nn.sigmoid(gate) * (0.5 * up)).astype(BF16)
    o_ref[...] += jnp.dot(act, wd_ref[...].astype(BF16), preferred_element_type=F32)

    if final_norm:
        @pl.when(j == pl.num_programs(1) - 1)
        def _():
            o_ref[...] = _rms(o_ref[...], fg_ref[...])


def _ffn(x, g, wg, wu, wd, fg, final_norm):
    t, d = x.shape
    dff = wg.shape[1]
    grid = (t // FFN_TM, dff // FFN_TF)
    return pl.pallas_call(
        functools.partial(_ffn_kernel, final_norm=final_norm),
        grid=grid,
        in_specs=[
            pl.BlockSpec((FFN_TM, d), lambda i, j: (i, 0)),
            pl.BlockSpec((1, d), lambda i, j: (0, 0)),
            pl.BlockSpec((d, FFN_TF), lambda i, j: (0, j)),
            pl.BlockSpec((d, FFN_TF), lambda i, j: (0, j)),
            pl.BlockSpec((FFN_TF, d), lambda i, j: (j, 0)),
            pl.BlockSpec((1, d), lambda i, j: (0, 0)),
        ],
        out_specs=pl.BlockSpec((FFN_TM, d), lambda i, j: (i, 0)),
        out_shape=jax.ShapeDtypeStruct((t, d), F32),
        scratch_shapes=[pltpu.VMEM((FFN_TM, d), BF16)],
        compiler_params=pltpu.CompilerParams(
            dimension_semantics=("parallel", "arbitrary"), vmem_limit_bytes=VMEM_LIMIT),
        name="ffn_final" if final_norm else "ffn",
    )(x, g, wg, wu, wd, fg)


INP_TM = 512
INP_SUB = 2


def _rope_dup(r, cos_t, sin_t):
    return r * cos_t + pltpu.roll(r, LANES - QK_ROPE // 2, axis=1) * sin_t


def _inproj_kernel(x_ref, g_ref, w_ref, qan_ref, wqup_ref, kvan_ref, wkvup_ref, qn_ref, kn_ref,
                   cos_ref, sin_ref, u_ref, q_ref, k_ref, v_ref, wbf_ref):
    sub = INP_TM // INP_SUB
    inv_head = 1.0 / QK_HEAD
    d_in = w_ref.shape[1]

    @pl.when(pl.program_id(0) == 0)
    def _():
        wbf_ref[:, :d_in] = w_ref[...].astype(BF16)
        wbf_ref[:, d_in:] = w_ref[:, d_in - QK_ROPE:].astype(BF16)

    for sb in range(INP_SUB):
        rows = slice(sb * sub, (sb + 1) * sub)
        h = _rms(x_ref[rows, :], g_ref[...]).astype(BF16)
        u_ref[rows, :] = jnp.dot(h, wbf_ref[:, :SSM_WIDTH], preferred_element_type=F32)
        lat = jnp.dot(h, wbf_ref[:, SSM_WIDTH:], preferred_element_type=F32)
        ql = lat[:, :Q_LORA]
        kvl = lat[:, Q_LORA:Q_LORA + KV_LORA]
        kpe = lat[:, Q_LORA + KV_LORA:]
        q = jnp.dot(_rms(ql, qan_ref[...]).astype(BF16), wqup_ref[...], preferred_element_type=F32)
        kv = jnp.dot(_rms(kvl, kvan_ref[...]).astype(BF16), wkvup_ref[...], preferred_element_type=F32)
        cos_t = cos_ref[rows, :]
        sin_t = sin_ref[rows, :]
        qn = qn_ref[...]
        kn = kn_ref[...]
        kpe_rot = _rope_dup(kpe * kn[:, LANES:], cos_t, sin_t)
        kpe_ssq = 0.5 * jnp.sum(kpe * kpe, axis=-1, keepdims=True)
        for hd in range(N_HEADS):
            qno = q[:, hd * HEAD_PAD:hd * HEAD_PAD + QK_NOPE]
            qro = q[:, hd * HEAD_PAD + QK_NOPE:(hd + 1) * HEAD_PAD]
            r = lax.rsqrt(jnp.sum(qno * qno + 0.5 * (qro * qro), axis=-1, keepdims=True) * inv_head + EPS)
            q_ref[hd, rows, :LANES] = (qno * r * qn[:, :LANES]).astype(BF16)
            q_ref[hd, rows, LANES:] = _rope_dup(qro * r * qn[:, LANES:], cos_t, sin_t).astype(BF16)
            kno = kv[:, hd * HEAD_PAD:hd * HEAD_PAD + QK_NOPE]
            r = lax.rsqrt((jnp.sum(kno * kno, axis=-1, keepdims=True) + kpe_ssq) * inv_head + EPS)
            k_ref[hd, rows, :LANES] = (kno * r * kn[:, :LANES]).astype(BF16)
            k_ref[hd, rows, LANES:] = (kpe_rot * r).astype(BF16)
            v_ref[hd, rows, :] = kv[:, hd * HEAD_PAD + QK_NOPE:(hd + 1) * HEAD_PAD].astype(BF16)


def _inproj(x, g, w, qan, wqup, kvan, wkvup, qn, kn, cos_t, sin_t):
    t, d = x.shape
    tm = INP_TM
    row = lambda i: (i, 0)
    head_row = lambda i: (0, i, 0)
    return pl.pallas_call(
        _inproj_kernel,
        grid=(t // tm,),
        in_specs=[
            pl.BlockSpec((tm, d), row),
            _const_spec(g.shape), _const_spec(w.shape), _const_spec(qan.shape), _const_spec(wqup.shape),
            _const_spec(kvan.shape), _const_spec(wkvup.shape), _const_spec(qn.shape), _const_spec(kn.shape),
            pl.BlockSpec((tm, LANES), row),
            pl.BlockSpec((tm, LANES), row),
        ],
        out_specs=[
            pl.BlockSpec((tm, SSM_WIDTH), row),
            pl.BlockSpec((N_HEADS, tm, HEAD_PAD), head_row),
            pl.BlockSpec((N_HEADS, tm, HEAD_PAD), head_row),
            pl.BlockSpec((N_HEADS, tm, V_HEAD), head_row),
        ],
        out_shape=[
            jax.ShapeDtypeStruct((t, SSM_WIDTH), F32),
            jax.ShapeDtypeStruct((N_HEADS, t, HEAD_PAD), BF16),
            jax.ShapeDtypeStruct((N_HEADS, t, HEAD_PAD), BF16),
            jax.ShapeDtypeStruct((N_HEADS, t, V_HEAD), BF16),
        ],
        scratch_shapes=[pltpu.VMEM((d, w.shape[1] + QK_ROPE), BF16)],
        compiler_params=pltpu.CompilerParams(
            dimension_semantics=("arbitrary",), vmem_limit_bytes=VMEM_LIMIT),
        name="in_proj",
    )(x, g, w, qan, wqup, kvan, wkvup, qn, kn, cos_t, sin_t)


S5_TC = 256
S5_KT = SSM_WIDTH // MXU_DIM
S5_SW = MXU_DIM // SSM_GROUP * SSM_STATE
S5_TILES = 2 * S5_SW // LANES
S5_ROWS = S5_KT * S5_TILES * SUBLANES
S5_PITCH = (2 * S5_KT + 1) * SUBLANES


def _s5_kernel(u_ref, wb_ref, lr_ref, li_ref, wc_ref, d_ref, wglu_ref, bglu_ref, on_ref,
               y_ref, bu_ref, st_ref, carry_ref):
    rg = S5_TC // SUBLANES

    @pl.when(pl.program_id(1) == 0)
    def _():
        carry_ref[...] = jnp.zeros_like(carry_ref)

    u = u_ref[...]
    ub = u.astype(BF16)
    for j in range(S5_KT):
        res = jnp.dot(ub[:, j * MXU_DIM:(j + 1) * MXU_DIM], wb_ref[j], preferred_element_type=F32)
        for c in range(S5_TILES):
            row0 = (j * S5_TILES + c) * SUBLANES
            bu_ref[:, row0:row0 + SUBLANES, :] = res[:, c * LANES:(c + 1) * LANES].reshape(rg, SUBLANES, LANES)

    lam = [(lr_ref[j], li_ref[j]) for j in range(S5_KT)]
    half = S5_TILES // 2 * SUBLANES

    def step(r, carry):
        for s in range(SUBLANES):
            nxt = []
            for j in range(S5_KT):
                base = j * S5_TILES * SUBLANES + s
                idx_r = pl.ds(base, SUBLANES, stride=SUBLANES)
                idx_i = pl.ds(base + half, SUBLANES, stride=SUBLANES)
                lr, li = lam[j]
                sr, si = carry[2 * j], carry[2 * j + 1]
                nr = lr * sr - li * si + bu_ref[r, idx_r, :]
                ni = lr * si + li * sr + bu_ref[r, idx_i, :]
                row0 = pl.multiple_of(r * (SUBLANES * S5_PITCH), SUBLANES) + s * S5_PITCH
                st_ref[pl.ds(row0 + 2 * j * SUBLANES, SUBLANES), :] = nr
                st_ref[pl.ds(row0 + (2 * j + 1) * SUBLANES, SUBLANES), :] = ni
                nxt += [nr, ni]
            carry = tuple(nxt)
        return carry

    init = tuple(carry_ref[v] for v in range(2 * S5_KT))
    fin = lax.fori_loop(0, rg, step, init)
    for v in range(2 * S5_KT):
        carry_ref[v] = fin[v]

    ys = []
    for j in range(S5_KT):
        tiles = [st_ref[pl.ds(2 * j * SUBLANES + c, S5_TC, stride=S5_PITCH), :] for c in range(S5_TILES)]
        sj = jnp.concatenate(tiles, axis=1).astype(BF16)
        ys.append(jnp.dot(sj, wc_ref[j], preferred_element_type=F32))
    y = jnp.concatenate(ys, axis=1) + d_ref[...] * u
    y = jax.nn.gelu(y)
    y = y * jax.nn.sigmoid(jnp.dot(y.astype(BF16), wglu_ref[...], preferred_element_type=F32) + bglu_ref[...])
    y_ref[...] = _rms(y, on_ref[...]).astype(BF16)


def _s5(u, wb, lr, li, wc, d, wglu, bglu, on, batch, seq):
    nchunk = seq // S5_TC
    row = lambda b, c: (b * nchunk + c, 0)
    rg = S5_TC // SUBLANES
    return pl.pallas_call(
        _s5_kernel,
        grid=(batch, nchunk),
        in_specs=[
            pl.BlockSpec((S5_TC, SSM_WIDTH), row),
            _const_spec(wb.shape), _const_spec(lr.shape), _const_spec(li.shape), _const_spec(wc.shape),
            _const_spec(d.shape), _const_spec(wglu.shape), _const_spec(bglu.shape), _const_spec(on.shape),
        ],
        out_specs=pl.BlockSpec((S5_TC, SSM_WIDTH), row),
        out_shape=jax.ShapeDtypeStruct((batch * seq, SSM_WIDTH), BF16),
        scratch_shapes=[
            pltpu.VMEM((rg, S5_ROWS, LANES), F32),
            pltpu.VMEM((S5_TC * S5_PITCH, LANES), F32),
            pltpu.VMEM((2 * S5_KT, SUBLANES, LANES), F32),
        ],
        compiler_params=pltpu.CompilerParams(
            dimension_semantics=("parallel", "arbitrary"), vmem_limit_bytes=VMEM_LIMIT),
        name="s5",
    )(u, wb, lr, li, wc, d, wglu, bglu, on)


ATT_TQ = 512


def _attn_kernel(q_ref, k_ref, v_ref, bias_ref, o_ref):
    seq = q_ref.shape[0]
    scale = QK_HEAD ** -0.5 * math.log2(math.e)
    nt = (((1,), (1,)), ((), ()))
    bias = bias_ref[...]
    for qi in range(seq // ATT_TQ):
        r0 = qi * ATT_TQ
        q = q_ref[r0:r0 + ATT_TQ, :]
        s_d = lax.dot_general(q, k_ref[r0:r0 + ATT_TQ, :], nt, preferred_element_type=F32) * scale + bias
        m = jnp.max(s_d, axis=-1, keepdims=True)
        if qi > 0:
            s_o = lax.dot_general(q, k_ref[:r0, :], nt, preferred_element_type=F32) * scale
            m = jnp.maximum(m, jnp.max(s_o, axis=-1, keepdims=True))
            p_o = jnp.exp2(s_o - m)
            l = jnp.sum(p_o, axis=-1, keepdims=True)
            acc = jnp.dot(p_o.astype(BF16), v_ref[:r0, :], preferred_element_type=F32)
        p_d = jnp.exp2(s_d - m)
        l_d = jnp.sum(p_d, axis=-1, keepdims=True)
        acc_d = jnp.dot(p_d.astype(BF16), v_ref[r0:r0 + ATT_TQ, :], preferred_element_type=F32)
        if qi > 0:
            l_d = l_d + l
            acc_d = acc_d + acc
        o_ref[r0:r0 + ATT_TQ, :] = acc_d / l_d


def _attention(q, k, v, batch, seq):
    qc = lax.broadcasted_iota(jnp.int32, (ATT_TQ, ATT_TQ), 0) // CHUNK
    kc = lax.broadcasted_iota(jnp.int32, (ATT_TQ, ATT_TQ), 1) // CHUNK
    bias = jnp.where(kc <= qc, 0.0, -jnp.inf).astype(F32)
    return pl.pallas_call(
        _attn_kernel,
        grid=(batch, N_HEADS),
        in_specs=[
            pl.BlockSpec((None, seq, HEAD_PAD), lambda b, h: (h, b, 0)),
            pl.BlockSpec((None, seq, HEAD_PAD), lambda b, h: (h, b, 0)),
            pl.BlockSpec((None, seq, V_HEAD), lambda b, h: (h, b, 0)),
            _const_spec(bias.shape),
        ],
        out_specs=pl.BlockSpec((seq, V_HEAD), lambda b, h: (b, h)),
        out_shape=jax.ShapeDtypeStruct((batch * seq, ATTN_WIDTH), F32),
        compiler_params=pltpu.CompilerParams(
            dimension_semantics=("parallel", "parallel"), vmem_limit_bytes=VMEM_LIMIT),
        name="attention",
    )(q, k, v, bias)


OUT_TM = 512


def _outproj_kernel(x_ref, ys_ref, ya_ref, an_ref, w_ref, o_ref, wbf_ref):
    @pl.when(pl.program_id(0) == 0)
    def _():
        wbf_ref[...] = w_ref[...].astype(BF16)

    ya = _rms(ya_ref[...], an_ref[...]).astype(BF16)
    o_ref[...] = (x_ref[...]
                  + jnp.dot(ys_ref[...], wbf_ref[:SSM_WIDTH, :], preferred_element_type=F32)
                  + jnp.dot(ya, wbf_ref[SSM_WIDTH:, :], preferred_element_type=F32))


def _outproj(x, ys, ya, an, w):
    t, d = x.shape
    tm = OUT_TM
    row = lambda i: (i, 0)
    return pl.pallas_call(
        _outproj_kernel,
        grid=(t // tm,),
        in_specs=[
            pl.BlockSpec((tm, d), row),
            pl.BlockSpec((tm, SSM_WIDTH), row),
            pl.BlockSpec((tm, ATTN_WIDTH), row),
            _const_spec(an.shape), _const_spec(w.shape),
        ],
        out_specs=pl.BlockSpec((tm, d), row),
        out_shape=jax.ShapeDtypeStruct((t, d), F32),
        scratch_shapes=[pltpu.VMEM(w.shape, BF16)],
        compiler_params=pltpu.CompilerParams(
            dimension_semantics=("arbitrary",), vmem_limit_bytes=VMEM_LIMIT),
        name="out_proj",
    )(x, ys, ya, an, w)


def _s5_params(log_step, a_re, a_im, b_re, b_im, c_re, c_im):
    g16 = MXU_DIM // SSM_GROUP
    dt = jnp.exp(log_step)[:, None]
    mag = jnp.exp(a_re * dt)
    ang = a_im * dt
    lr, li = mag * jnp.cos(ang), mag * jnp.sin(ang)
    den = a_re * a_re + a_im * a_im
    fr = ((lr - 1.0) * a_re + li * a_im) / den
    fi = (li * a_re - (lr - 1.0) * a_im) / den
    bb_r = fr[..., None] * b_re - fi[..., None] * b_im
    bb_i = fr[..., None] * b_im + fi[..., None] * b_re
    ch_grp = jnp.arange(MXU_DIM) // SSM_GROUP
    st_grp = jnp.arange(S5_SW) // SSM_STATE

    def bd_in(b):
        rows = b.transpose(0, 2, 1).reshape(S5_KT, MXU_DIM, SSM_STATE)
        return jnp.where(ch_grp[:, None] == st_grp[None, :], jnp.tile(rows, (1, 1, g16)), 0.0)

    def bd_out(c):
        rows = c.transpose(0, 2, 1).reshape(S5_KT, S5_SW, SSM_GROUP)
        return jnp.where(st_grp[:, None] == ch_grp[None, :], jnp.tile(rows, (1, 1, g16)), 0.0)

    wb = jnp.concatenate([bd_in(bb_r), bd_in(bb_i)], axis=2).astype(BF16)
    wc = jnp.concatenate([bd_out(c_re), bd_out(-c_im)], axis=1).astype(BF16)
    lam_r = lr.reshape(S5_KT, SUBLANES, LANES)
    lam_i = li.reshape(S5_KT, SUBLANES, LANES)
    return wb, lam_r, lam_i, wc


def _dup_rope(w):
    k = w.shape[0]
    w = w.reshape(k, -1, QK_HEAD)
    return jnp.concatenate([w, w[:, :, QK_NOPE:]], axis=2).reshape(k, -1)


def kernel(x, positions, ffn1_norm, ffn1_w_gate, ffn1_w_up, ffn1_w_down, mix_norm, w_in, ssm_log_step, ssm_a_re, ssm_a_im, ssm_b_re, ssm_b_im, ssm_c_re, ssm_c_im, ssm_d, ssm_w_glu, ssm_b_glu, mla_q_a_norm, mla_w_q_up, mla_kv_a_norm, mla_w_kv_up, mla_q_norm, mla_k_norm, ssm_out_norm, attn_out_norm, w_out, ffn2_norm, ffn2_w_gate, ffn2_w_up, ffn2_w_down, final_norm):
    batch, seq, d = x.shape
    depth = w_in.shape[0]
    t = batch * seq
    xt = x.reshape(t, d)

    inv_freq = ROPE_THETA ** (-jnp.arange(0, QK_ROPE, 2, dtype=F32) / QK_ROPE)
    half = QK_ROPE // 2
    lane = jnp.arange(LANES)
    ang = positions.astype(F32).reshape(t, 1) * jnp.tile(inv_freq, LANES // half)
    cos_t = jnp.where(lane < QK_ROPE, jnp.cos(ang), 0.0)
    sin_t = jnp.where(lane < half, -jnp.sin(ang), jnp.where(lane < QK_ROPE, jnp.sin(ang), 0.0))

    row = lambda v: v.reshape(1, -1)
    for l in range(depth):
        xt = _ffn(xt, row(ffn1_norm[l]), ffn1_w_gate[l], ffn1_w_up[l], ffn1_w_down[l], row(final_norm[l]),
                  final_norm=False)

        qn = _dup_rope(row(mla_q_norm[l]))
        kn = _dup_rope(row(mla_k_norm[l]))
        u, q, k, v = _inproj(
            xt, row(mix_norm[l]), w_in[l], row(mla_q_a_norm[l]), _dup_rope(mla_w_q_up[l]).astype(BF16),
            row(mla_kv_a_norm[l]), mla_w_kv_up[l].astype(BF16), qn, kn, cos_t, sin_t)

        wb, lam_r, lam_i, wc = _s5_params(ssm_log_step[l], ssm_a_re[l], ssm_a_im[l], ssm_b_re[l],
                                          ssm_b_im[l], ssm_c_re[l], ssm_c_im[l])
        ys = _s5(u, wb, lam_r, lam_i, wc, row(ssm_d[l]), ssm_w_glu[l].astype(BF16), row(ssm_b_glu[l]),
                 row(ssm_out_norm[l]), batch, seq)

        ya = _attention(q, k, v, batch, seq)

        xt = _outproj(xt, ys, ya, row(attn_out_norm[l]), w_out[l])

        xt = _ffn(xt, row(ffn2_norm[l]), ffn2_w_gate[l], ffn2_w_up[l], ffn2_w_down[l], row(final_norm[l]),
                  final_norm=True)
    return xt.reshape(batch, seq, d)
```

```python
import functools
import math

import jax
import jax.numpy as jnp
import numpy as np
from jax import lax
from jax.experimental import pallas as pl
from jax.experimental.pallas import tpu as pltpu

D_MODEL = 2048
CHUNK = 64
SSM_WIDTH = 1024
SSM_GROUP = 16
SSM_GROUPS = 64
SSM_STATE = 64
N_HEADS = 8
QK_NOPE = 128
QK_ROPE = 64
QK_HEAD = QK_NOPE + QK_ROPE
V_HEAD = 128
ATTN_WIDTH = N_HEADS * V_HEAD
Q_LORA = 512
KV_LORA = 256
D_FF = 5632
ROPE_THETA = 10000.0
EPS = 1e-6

LANES = 128
SUBLANES = 8
MXU_DIM = 256
HEAD_PAD = 2 * LANES
VMEM_LIMIT = 60 * 1024 * 1024

BF16 = jnp.bfloat16
F32 = jnp.float32


def _rms(xf, g):
    return xf * lax.rsqrt(jnp.mean(xf * xf, axis=-1, keepdims=True) + EPS) * g


def _const_spec(shape):
    nd = len(shape)
    return pl.BlockSpec(shape, lambda *_: (0,) * nd, pipeline_mode=pl.Buffered(1))


FFN_TM = 1024
FFN_TF = 256


def _ffn_kernel(x_ref, g_ref, wg_ref, wu_ref, wd_ref, fg_ref, o_ref, h_ref, *, final_norm):
    j = pl.program_id(1)

    @pl.when(j == 0)
    def _():
        x = x_ref[...]
        h_ref[...] = _rms(x, g_ref[...]).astype(BF16)
        o_ref[...] = x

    h = h_ref[...]
    gate = jnp.dot(h, wg_ref[...].astype(BF16), preferred_element_type=F32)
    up = jnp.dot(h, wu_ref[...].astype(BF16), preferred_element_type=F32)
    act = (gate * jax.nn.sigmoid(gate) * (0.5 * up)).astype(BF16)
    o_ref[...] += jnp.dot(act, wd_ref[...].astype(BF16), preferred_element_type=F32)

    if final_norm:
        @pl.when(j == pl.num_programs(1) - 1)
        def _():
            o_ref[...] = _rms(o_ref[...], fg_ref[...])


def _ffn(x, g, wg, wu, wd, fg, final_norm):
    t, d = x.shape
    dff = wg.shape[1]
    grid = (t // FFN_TM, dff // FFN_TF)
    return pl.pallas_call(
        functools.partial(_ffn_kernel, final_norm=final_norm),
        grid=grid,
        in_specs=[
            pl.BlockSpec((FFN_TM, d), lambda i, j: (i, 0)),
            pl.BlockSpec((1, d), lambda i, j: (0, 0)),
            pl.BlockSpec((d, FFN_TF), lambda i, j: (0, j)),
            pl.BlockSpec((d, FFN_TF), lambda i, j: (0, j)),
            pl.BlockSpec((FFN_TF, d), lambda i, j: (j, 0)),
            pl.BlockSpec((1, d), lambda i, j: (0, 0)),
        ],
        out_specs=pl.BlockSpec((FFN_TM, d), lambda i, j: (i, 0)),
        out_shape=jax.ShapeDtypeStruct((t, d), F32),
        scratch_shapes=[pltpu.VMEM((FFN_TM, d), BF16)],
        compiler_params=pltpu.CompilerParams(
            dimension_semantics=("parallel", "arbitrary"), vmem_limit_bytes=VMEM_LIMIT),
        name="ffn_final" if final_norm else "ffn",
    )(x, g, wg, wu, wd, fg)


INP_TM = 512
INP_SUB = 2


def _rope_dup(r, cos_t, sin_t):
    return r * cos_t + pltpu.roll(r, LANES - QK_ROPE // 2, axis=1) * sin_t


def _inproj_kernel(x_ref, g_ref, w_ref, qan_ref, wqup_ref, kvan_ref, wkvup_ref, qn_ref, kn_ref,
                   cos_ref, sin_ref, u_ref, q_ref, k_ref, v_ref, wbf_ref):
    sub = INP_TM // INP_SUB
    inv_head = 1.0 / QK_HEAD
    d_in = w_ref.shape[1]

    @pl.when(pl.program_id(0) == 0)
    def _():
        wbf_ref[:, :d_in] = w_ref[...].astype(BF16)
        wbf_ref[:, d_in:] = w_ref[:, d_in - QK_ROPE:].astype(BF16)

    for sb in range(INP_SUB):
        rows = slice(sb * sub, (sb + 1) * sub)
        h = _rms(x_ref[rows, :], g_ref[...]).astype(BF16)
        u_ref[rows, :] = jnp.dot(h, wbf_ref[:, :SSM_WIDTH], preferred_element_type=F32)
        lat = jnp.dot(h, wbf_ref[:, SSM_WIDTH:], preferred_element_type=F32)
        ql = lat[:, :Q_LORA]
        kvl = lat[:, Q_LORA:Q_LORA + KV_LORA]
        kpe = lat[:, Q_LORA + KV_LORA:]
        q = jnp.dot(_rms(ql, qan_ref[...]).astype(BF16), wqup_ref[...], preferred_element_type=F32)
        kv = jnp.dot(_rms(kvl, kvan_ref[...]).astype(BF16), wkvup_ref[...], preferred_element_type=F32)
        cos_t = cos_ref[rows, :]
        sin_t = sin_ref[rows, :]
        qn = qn_ref[...]
        kn = kn_ref[...]
        kpe_rot = _rope_dup(kpe * kn[:, LANES:], cos_t, sin_t)
        kpe_ssq = 0.5 * jnp.sum(kpe * kpe, axis=-1, keepdims=True)
        for hd in range(N_HEADS):
            qno = q[:, hd * HEAD_PAD:hd * HEAD_PAD + QK_NOPE]
            qro = q[:, hd * HEAD_PAD + QK_NOPE:(hd + 1) * HEAD_PAD]
            r = lax.rsqrt(jnp.sum(qno * qno + 0.5 * (qro * qro), axis=-1, keepdims=True) * inv_head + EPS)
            q_ref[hd, rows, :LANES] = (qno * r * qn[:, :LANES]).astype(BF16)
            q_ref[hd, rows, LANES:] = _rope_dup(qro * r * qn[:, LANES:], cos_t, sin_t).astype(BF16)
            kno = kv[:, hd * HEAD_PAD:hd * HEAD_PAD + QK_NOPE]
            r = lax.rsqrt((jnp.sum(kno * kno, axis=-1, keepdims=True) + kpe_ssq) * inv_head + EPS)
            k_ref[hd, rows, :LANES] = (kno * r * kn[:, :LANES]).astype(BF16)
            k_ref[hd, rows, LANES:] = (kpe_rot * r).astype(BF16)
            v_ref[hd, rows, :] = kv[:, hd * HEAD_PAD + QK_NOPE:(hd + 1) * HEAD_PAD].astype(BF16)


def _inproj(x, g, w, qan, wqup, kvan, wkvup, qn, kn, cos_t, sin_t):
    t, d = x.shape
    tm = INP_TM
    row = lambda i: (i, 0)
    head_row = lambda i: (0, i, 0)
    return pl.pallas_call(
        _inproj_kernel,
        grid=(t // tm,),
        in_specs=[
            pl.BlockSpec((tm, d), row),
            _const_spec(g.shape), _const_spec(w.shape), _const_spec(qan.shape), _const_spec(wqup.shape),
            _const_spec(kvan.shape), _const_spec(wkvup.shape), _const_spec(qn.shape), _const_spec(kn.shape),
            pl.BlockSpec((tm, LANES), row),
            pl.BlockSpec((tm, LANES), row),
        ],
        out_specs=[
            pl.BlockSpec((tm, SSM_WIDTH), row),
            pl.BlockSpec((N_HEADS, tm, HEAD_PAD), head_row),
            pl.BlockSpec((N_HEADS, tm, HEAD_PAD), head_row),
            pl.BlockSpec((N_HEADS, tm, V_HEAD), head_row),
        ],
        out_shape=[
            jax.ShapeDtypeStruct((t, SSM_WIDTH), F32),
            jax.ShapeDtypeStruct((N_HEADS, t, HEAD_PAD), BF16),
            jax.ShapeDtypeStruct((N_HEADS, t, HEAD_PAD), BF16),
            jax.ShapeDtypeStruct((N_HEADS, t, V_HEAD), BF16),
        ],
        scratch_shapes=[pltpu.VMEM((d, w.shape[1] + QK_ROPE), BF16)],
        compiler_params=pltpu.CompilerParams(
            dimension_semantics=("arbitrary",), vmem_limit_bytes=VMEM_LIMIT),
        name="in_proj",
    )(x, g, w, qan, wqup, kvan, wkvup, qn, kn, cos_t, sin_t)


S5_GF = SUBLANES
S5_RC = 128
S5_FR = S5_GF * S5_RC
S5_LT = SSM_WIDTH // LANES
S5_BLK = LANES // SSM_GROUP
S5_NP = S5_LT // 2
S5_UP = S5_RC + SUBLANES
S5_NV = S5_BLK * S5_NP * 2 // SUBLANES
S5_PITCH = (S5_NV + 1) * SUBLANES
S5_UNROLL = 8


def _frame_exchange(x, res, inverse):
    def skew(v):
        return pltpu.roll(v, 0, 1, stride=1, stride_axis=0)

    y = x if inverse else skew(x)
    for bit in (1, 2, 4):
        y = jnp.where((res & bit) != 0, pltpu.roll(y, SUBLANES - bit, 0), y)
    return skew(y) if inverse else y


def _exchange_maps():
    r = np.arange(S5_BLK)[:, None]
    q, c = np.divmod(np.arange(LANES)[None, :], S5_BLK)
    in_a = (r + c) % S5_GF
    in_ch = (q - (c < in_a)) % SSM_GROUP
    out_a = (r - c) % S5_GF
    out_ch = (q + (c + out_a >= S5_BLK)) % SSM_GROUP
    return in_a, in_ch, out_a, out_ch


def _s5g_kernel(u_ref, w1_ref, w2_ref, w3_ref, lr_ref, li_ref, d_ref, wglu_ref, bglu_ref, on_ref,
                y_ref, ex_ref, ey_ref, cs_ref, sp_ref, yn_ref, carry_ref):
    @pl.when(pl.program_id(1) == 0)
    def _():
        carry_ref[...] = jnp.zeros_like(carry_ref)

    res = lax.broadcasted_iota(jnp.int32, (SUBLANES, LANES), 1) % S5_BLK

    def fwd(k, carry):
        r0 = pl.multiple_of(k * S5_GF, S5_GF)
        for m in range(S5_LT):
            x = u_ref[pl.ds(r0, S5_GF), m * LANES:(m + 1) * LANES]
            ex_ref[m, pl.ds(k, S5_BLK, stride=S5_UP), :] = _frame_exchange(x, res, False)
        return carry

    lax.fori_loop(0, S5_RC, fwd, 0, unroll=S5_UNROLL)

    def lhs_u(bl, p):
        r = (-bl) % S5_BLK
        rows = slice(r * S5_UP, r * S5_UP + S5_RC)
        return jnp.concatenate([ex_ref[2 * p, rows, :], ex_ref[2 * p + 1, rows, :]], axis=1).astype(BF16)

    for bl in range(S5_BLK):
        for p in range(S5_NP):
            c = jnp.dot(lhs_u(bl, p), w1_ref[bl, p], preferred_element_type=F32)
            t = bl * S5_NP + p
            for part in range(2):
                t0 = (part * S5_BLK * S5_NP + t) * SUBLANES
                cs_ref[:, t0:t0 + SUBLANES, :] = (c[:, part * LANES:(part + 1) * LANES]
                                                  .reshape(S5_RC // SUBLANES, SUBLANES, LANES))

    lam = [(lr_ref[v], li_ref[v]) for v in range(S5_NV // 2)]
    nre = S5_NV // 2

    def step(kr, carry):
        for s in range(SUBLANES):
            row0 = pl.multiple_of(kr * (SUBLANES * S5_PITCH), SUBLANES) + s * S5_PITCH
            nxt_r, nxt_i = [], []
            for v in range(nre):
                sr, si = carry[v], carry[nre + v]
                sp_ref[pl.ds(row0 + v * SUBLANES, SUBLANES), :] = sr
                sp_ref[pl.ds(row0 + (nre + v) * SUBLANES, SUBLANES), :] = si
                cr = cs_ref[kr, pl.ds(v * SUBLANES * SUBLANES + s, SUBLANES, stride=SUBLANES), :]
                ci = cs_ref[kr, pl.ds((nre + v) * SUBLANES * SUBLANES + s, SUBLANES, stride=SUBLANES), :]
                lr, li = lam[v]
                nxt_r.append(lr * sr - li * si + cr)
                nxt_i.append(lr * si + li * sr + ci)
            carry = tuple(nxt_r + nxt_i)
        return carry

    init = tuple(carry_ref[v] for v in range(S5_NV))
    fin = lax.fori_loop(0, S5_RC // SUBLANES, step, init)
    for v in range(S5_NV):
        carry_ref[v] = fin[v]

    for bl in range(S5_BLK):
        for p in range(S5_NP):
            v, i = divmod(bl * S5_NP + p, SUBLANES)
            s_re = sp_ref[pl.ds(v * SUBLANES + i, S5_RC, stride=S5_PITCH), :]
            s_im = sp_ref[pl.ds((nre + v) * SUBLANES + i, S5_RC, stride=S5_PITCH), :]
            lhs_s = jnp.concatenate([s_re, s_im], axis=1).astype(BF16)
            y = (jnp.dot(lhs_s, w2_ref[bl, p], preferred_element_type=F32)
                 + jnp.dot(lhs_u(bl, p), w3_ref[bl, p], preferred_element_type=F32))
            rows = slice(bl * S5_UP, bl * S5_UP + S5_RC)
            ey_ref[2 * p, rows, :] = y[:, :LANES]
            ey_ref[2 * p + 1, rows, :] = y[:, LANES:]

    def inv(k, carry):
        r0 = pl.multiple_of(k * S5_GF, S5_GF)
        for m in range(S5_LT):
            x = ey_ref[m, pl.ds(k, S5_BLK, stride=S5_UP), :]
            yn_ref[pl.ds(r0, S5_GF), m * LANES:(m + 1) * LANES] = _frame_exchange(x, res, True)
        return carry

    lax.fori_loop(0, S5_RC, inv, 0, unroll=S5_UNROLL)

    for rb in range(S5_FR // MXU_DIM):
        rows = slice(rb * MXU_DIM, (rb + 1) * MXU_DIM)
        y = yn_ref[rows, :] + d_ref[...] * u_ref[rows, :]
        y = jax.nn.gelu(y)
        gate = jnp.dot(y.astype(BF16), wglu_ref[...], preferred_element_type=F32) + bglu_ref[...]
        y_ref[rows, :] = _rms(y * jax.nn.sigmoid(gate), on_ref[...]).astype(BF16)


def _s5g(u, w1, w2, w3, lr, li, d, wglu, bglu, on, batch, seq):
    nchunk = seq // S5_FR
    row = lambda b, c: (b * nchunk + c, 0)
    return pl.pallas_call(
        _s5g_kernel,
        grid=(batch, nchunk),
        in_specs=[
            pl.BlockSpec((S5_FR, SSM_WIDTH), row),
            _const_spec(w1.shape), _const_spec(w2.shape), _const_spec(w3.shape),
            _const_spec(lr.shape), _const_spec(li.shape),
            _const_spec(d.shape), _const_spec(wglu.shape), _const_spec(bglu.shape), _const_spec(on.shape),
        ],
        out_specs=pl.BlockSpec((S5_FR, SSM_WIDTH), row),
        out_shape=jax.ShapeDtypeStruct((batch * seq, SSM_WIDTH), BF16),
        scratch_shapes=[
            pltpu.VMEM((S5_LT, S5_BLK * S5_UP, LANES), F32),
            pltpu.VMEM((S5_LT, S5_BLK * S5_UP, LANES), F32),
            pltpu.VMEM((S5_RC // SUBLANES, S5_NV * SUBLANES * SUBLANES, LANES), F32),
            pltpu.VMEM((S5_RC * S5_PITCH, LANES), F32),
            pltpu.VMEM((S5_FR, SSM_WIDTH), F32),
            pltpu.VMEM((S5_NV, SUBLANES, LANES), F32),
        ],
        compiler_params=pltpu.CompilerParams(
            dimension_semantics=("parallel", "arbitrary"), vmem_limit_bytes=VMEM_LIMIT),
        name="s5",
    )(u, w1, w2, w3, lr, li, d, wglu, bglu, on)


def _s5g_params(log_step, a_re, a_im, b_re, b_im, c_re, c_im):
    G, N, C, GF = SSM_GROUPS, SSM_STATE, SSM_GROUP, S5_GF
    hi = lax.Precision.HIGHEST
    dt = jnp.exp(log_step)[:, None]
    mag = jnp.exp(a_re * dt)
    ang = a_im * dt
    lr, li = mag * jnp.cos(ang), mag * jnp.sin(ang)
    den = a_re * a_re + a_im * a_im
    fr = ((lr - 1.0) * a_re + li * a_im) / den
    fi = (li * a_re - (lr - 1.0) * a_im) / den
    bb_r = fr[..., None] * b_re - fi[..., None] * b_im
    bb_i = fr[..., None] * b_im + fi[..., None] * b_re
    pr, pi = [jnp.ones_like(lr)], [jnp.zeros_like(lr)]
    for _ in range(GF):
        pr, pi = pr + [pr[-1] * lr - pi[-1] * li], pi + [pr[-1] * li + pi[-1] * lr]
    pr, pi = jnp.stack(pr), jnp.stack(pi)
    e_r = pr[..., None] * bb_r - pi[..., None] * bb_i
    e_i = pr[..., None] * bb_i + pi[..., None] * bb_r
    q_r = c_re * pr[:, :, None, :] - c_im * pi[:, :, None, :]
    q_i = c_re * pi[:, :, None, :] + c_im * pr[:, :, None, :]
    k3 = (jnp.einsum('dgcn,gni->dgci', q_r, bb_r, precision=hi)
          - jnp.einsum('dgcn,gni->dgci', q_i, bb_i, precision=hi))

    in_a, in_ch, out_a, out_ch = _exchange_maps()
    bl_of = np.arange(G) % S5_BLK
    in_a, in_ch = in_a[(-bl_of) % S5_BLK], in_ch[(-bl_of) % S5_BLK]
    out_a, out_ch = out_a[bl_of], out_ch[bl_of]
    gg = np.arange(G)[:, None]
    w1_r = e_r[GF - 1 - in_a, gg, :, in_ch]
    w1_i = e_i[GF - 1 - in_a, gg, :, in_ch]
    w2_r = q_r[out_a + 1, gg, out_ch].transpose(0, 2, 1)
    w2_i = -q_i[out_a + 1, gg, out_ch].transpose(0, 2, 1)
    lag = out_a[:, None, :] - in_a[:, :, None]
    w3 = jnp.where(lag >= 0,
                   k3[np.maximum(lag, 0), gg[:, :, None], out_ch[:, None, :], in_ch[:, :, None]], 0.0)

    def pairs(w):
        return w.reshape((S5_NP, 2, S5_BLK) + w.shape[1:]).transpose((2, 0, 1) + tuple(range(3, w.ndim + 2)))

    def tile(blocks):
        return jnp.concatenate([jnp.concatenate(r, axis=-1) for r in blocks], axis=-2).astype(BF16)

    a_r, a_i, b_r, b_i, w3p = pairs(w1_r), pairs(w1_i), pairs(w2_r), pairs(w2_i), pairs(w3)
    z1, z2, z3 = jnp.zeros_like(a_r[:, :, 0]), jnp.zeros_like(b_r[:, :, 0]), jnp.zeros_like(w3p[:, :, 0])
    w1 = tile([[a_r[:, :, 0], z1, a_i[:, :, 0], z1], [z1, a_r[:, :, 1], z1, a_i[:, :, 1]]])
    w2 = tile([[b_r[:, :, 0], z2], [z2, b_r[:, :, 1]], [b_i[:, :, 0], z2], [z2, b_i[:, :, 1]]])
    w3t = tile([[w3p[:, :, 0], z3], [z3, w3p[:, :, 1]]])

    def state_vregs(x):
        return pairs(x).reshape(S5_NV // 2, SUBLANES, LANES)

    return w1, w2, w3t, state_vregs(pr[GF]), state_vregs(pi[GF])


ATT_TQ = 512


def _attn_kernel(q_ref, k_ref, v_ref, bias_ref, o_ref):
    seq = q_ref.shape[0]
    scale = QK_HEAD ** -0.5 * math.log2(math.e)
    nt = (((1,), (1,)), ((), ()))
    bias = bias_ref[...]
    for qi in range(seq // ATT_TQ):
        r0 = qi * ATT_TQ
        q = q_ref[r0:r0 + ATT_TQ, :]
        s_d = lax.dot_general(q, k_ref[r0:r0 + ATT_TQ, :], nt, preferred_element_type=F32) * scale + bias
        m = jnp.max(s_d, axis=-1, keepdims=True)
        if qi > 0:
            s_o = lax.dot_general(q, k_ref[:r0, :], nt, preferred_element_type=F32) * scale
            m = jnp.maximum(m, jnp.max(s_o, axis=-1, keepdims=True))
            p_o = jnp.exp2(s_o - m)
            l = jnp.sum(p_o, axis=-1, keepdims=True)
            acc = jnp.dot(p_o.astype(BF16), v_ref[:r0, :], preferred_element_type=F32)
        p_d = jnp.exp2(s_d - m)
        l_d = jnp.sum(p_d, axis=-1, keepdims=True)
        acc_d = jnp.dot(p_d.astype(BF16), v_ref[r0:r0 + ATT_TQ, :], preferred_element_type=F32)
        if qi > 0:
            l_d = l_d + l
            acc_d = acc_d + acc
        o_ref[r0:r0 + ATT_TQ, :] = acc_d / l_d


def _attention(q, k, v, batch, seq):
    qc = lax.broadcasted_iota(jnp.int32, (ATT_TQ, ATT_TQ), 0) // CHUNK
    kc = lax.broadcasted_iota(jnp.int32, (ATT_TQ, ATT_TQ), 1) // CHUNK
    bias = jnp.where(kc <= qc, 0.0, -jnp.inf).astype(F32)
    return pl.pallas_call(
        _attn_kernel,
        grid=(batch, N_HEADS),
        in_specs=[
            pl.BlockSpec((None, seq, HEAD_PAD), lambda b, h: (h, b, 0)),
            pl.BlockSpec((None, seq, HEAD_PAD), lambda b, h: (h, b, 0)),
            pl.BlockSpec((None, seq, V_HEAD), lambda b, h: (h, b, 0)),
            _const_spec(bias.shape),
        ],
        out_specs=pl.BlockSpec((seq, V_HEAD), lambda b, h: (b, h)),
        out_shape=jax.ShapeDtypeStruct((batch * seq, ATTN_WIDTH), F32),
        compiler_params=pltpu.CompilerParams(
            dimension_semantics=("parallel", "parallel"), vmem_limit_bytes=VMEM_LIMIT),
        name="attention",
    )(q, k, v, bias)


OUT_TM = 512


def _outproj_kernel(x_ref, ys_ref, ya_ref, an_ref, w_ref, o_ref, wbf_ref):
    @pl.when(pl.program_id(0) == 0)
    def _():
        wbf_ref[...] = w_ref[...].astype(BF16)

    ya = _rms(ya_ref[...], an_ref[...]).astype(BF16)
    o_ref[...] = (x_ref[...]
                  + jnp.dot(ys_ref[...], wbf_ref[:SSM_WIDTH, :], preferred_element_type=F32)
                  + jnp.dot(ya, wbf_ref[SSM_WIDTH:, :], preferred_element_type=F32))


def _outproj(x, ys, ya, an, w):
    t, d = x.shape
    tm = OUT_TM
    row = lambda i: (i, 0)
    return pl.pallas_call(
        _outproj_kernel,
        grid=(t // tm,),
        in_specs=[
            pl.BlockSpec((tm, d), row),
            pl.BlockSpec((tm, SSM_WIDTH), row),
            pl.BlockSpec((tm, ATTN_WIDTH), row),
            _const_spec(an.shape), _const_spec(w.shape),
        ],
        out_specs=pl.BlockSpec((tm, d), row),
        out_shape=jax.ShapeDtypeStruct((t, d), F32),
        scratch_shapes=[pltpu.VMEM(w.shape, BF16)],
        compiler_params=pltpu.CompilerParams(
            dimension_semantics=("arbitrary",), vmem_limit_bytes=VMEM_LIMIT),
        name="out_proj",
    )(x, ys, ya, an, w)


def _dup_rope(w):
    k = w.shape[0]
    w = w.reshape(k, -1, QK_HEAD)
    return jnp.concatenate([w, w[:, :, QK_NOPE:]], axis=2).reshape(k, -1)


def kernel(x, positions, ffn1_norm, ffn1_w_gate, ffn1_w_up, ffn1_w_down, mix_norm, w_in, ssm_log_step, ssm_a_re, ssm_a_im, ssm_b_re, ssm_b_im, ssm_c_re, ssm_c_im, ssm_d, ssm_w_glu, ssm_b_glu, mla_q_a_norm, mla_w_q_up, mla_kv_a_norm, mla_w_kv_up, mla_q_norm, mla_k_norm, ssm_out_norm, attn_out_norm, w_out, ffn2_norm, ffn2_w_gate, ffn2_w_up, ffn2_w_down, final_norm):
    batch, seq, d = x.shape
    depth = w_in.shape[0]
    t = batch * seq
    xt = x.reshape(t, d)

    inv_freq = ROPE_THETA ** (-jnp.arange(0, QK_ROPE, 2, dtype=F32) / QK_ROPE)
    half = QK_ROPE // 2
    lane = jnp.arange(LANES)
    ang = positions.astype(F32).reshape(t, 1) * jnp.tile(inv_freq, LANES // half)
    cos_t = jnp.where(lane < QK_ROPE, jnp.cos(ang), 0.0)
    sin_t = jnp.where(lane < half, -jnp.sin(ang), jnp.where(lane < QK_ROPE, jnp.sin(ang), 0.0))

    pos = np.arange(SSM_WIDTH)
    sm = pos // LANES * LANES + pos % S5_BLK * SSM_GROUP + pos % LANES // S5_BLK

    row = lambda v: v.reshape(1, -1)
    for l in range(depth):
        xt = _ffn(xt, row(ffn1_norm[l]), ffn1_w_gate[l], ffn1_w_up[l], ffn1_w_down[l], row(final_norm[l]),
                  final_norm=False)

        qn = _dup_rope(row(mla_q_norm[l]))
        kn = _dup_rope(row(mla_k_norm[l]))
        w = jnp.concatenate([w_in[l][:, :SSM_WIDTH][:, sm], w_in[l][:, SSM_WIDTH:]], axis=1)
        u, q, k, v = _inproj(
            xt, row(mix_norm[l]), w, row(mla_q_a_norm[l]), _dup_rope(mla_w_q_up[l]).astype(BF16),
            row(mla_kv_a_norm[l]), mla_w_kv_up[l].astype(BF16), qn, kn, cos_t, sin_t)

        w1, w2, w3, lam_r, lam_i = _s5g_params(ssm_log_step[l], ssm_a_re[l], ssm_a_im[l], ssm_b_re[l],
                                               ssm_b_im[l], ssm_c_re[l], ssm_c_im[l])
        ys = _s5g(u, w1, w2, w3, lam_r, lam_i, row(ssm_d[l].reshape(-1)[sm]),
                  ssm_w_glu[l][sm][:, sm].astype(BF16), row(ssm_b_glu[l][sm]), row(ssm_out_norm[l][sm]),
                  batch, seq)

        ya = _attention(q, k, v, batch, seq)

        wo = jnp.concatenate([w_out[l][:SSM_WIDTH][sm], w_out[l][SSM_WIDTH:]], axis=0)
        xt = _outproj(xt, ys, ya, row(attn_out_norm[l]), wo)

        xt = _ffn(xt, row(ffn2_norm[l]), ffn2_w_gate[l], ffn2_w_up[l], ffn2_w_down[l], row(final_norm[l]),
                  final_norm=True)
    return xt.reshape(batch, seq, d)
```

```python
import functools
import math

import jax
import jax.numpy as jnp
import numpy as np
from jax import lax
from jax.experimental import pallas as pl
from jax.experimental.pallas import tpu as pltpu

D_MODEL = 2048
CHUNK = 64
SSM_WIDTH = 1024
SSM_GROUP = 16
SSM_GROUPS = 64
SSM_STATE = 64
N_HEADS = 8
QK_NOPE = 128
QK_ROPE = 64
QK_HEAD = QK_NOPE + QK_ROPE
V_HEAD = 128
ATTN_WIDTH = N_HEADS * V_HEAD
Q_LORA = 512
KV_LORA = 256
D_FF = 5632
ROPE_THETA = 10000.0
EPS = 1e-6

LANES = 128
SUBLANES = 8
MXU_DIM = 256
HEAD_PAD = 2 * LANES
VMEM_LIMIT = 60 * 1024 * 1024

BF16 = jnp.bfloat16
F32 = jnp.float32


def _rms(xf, g):
    return xf * lax.rsqrt(jnp.mean(xf * xf, axis=-1, keepdims=True) + EPS) * g


def _const_spec(shape):
    nd = len(shape)
    return pl.BlockSpec(shape, lambda *_: (0,) * nd, pipeline_mode=pl.Buffered(1))


FFN_TM = 1024
FFN_TF = 256


def _ffn_kernel(x_ref, g_ref, wg_ref, wu_ref, wd_ref, fg_ref, o_ref, h_ref, *, final_norm):
    j = pl.program_id(1)

    @pl.when(j == 0)
    def _():
        x = x_ref[...]
        h_ref[...] = _rms(x, g_ref[...]).astype(BF16)
        o_ref[...] = x

    h = h_ref[...]
    gate = jnp.dot(h, wg_ref[...].astype(BF16), preferred_element_type=F32)
    up = jnp.dot(h, wu_ref[...].astype(BF16), preferred_element_type=F32)
    act = (gate * jax.nn.sigmoid(gate) * (0.5 * up)).astype(BF16)
    o_ref[...] += jnp.dot(act, wd_ref[...].astype(BF16), preferred_element_type=F32)

    if final_norm:
        @pl.when(j == pl.num_programs(1) - 1)
        def _():
            o_ref[...] = _rms(o_ref[...], fg_ref[...])


def _ffn(x, g, wg, wu, wd, fg, final_norm):
    t, d = x.shape
    dff = wg.shape[1]
    grid = (t // FFN_TM, dff // FFN_TF)
    return pl.pallas_call(
        functools.partial(_ffn_kernel, final_norm=final_norm),
        grid=grid,
        in_specs=[
            pl.BlockSpec((FFN_TM, d), lambda i, j: (i, 0)),
            pl.BlockSpec((1, d), lambda i, j: (0, 0)),
            pl.BlockSpec((d, FFN_TF), lambda i, j: (0, j)),
            pl.BlockSpec((d, FFN_TF), lambda i, j: (0, j)),
            pl.BlockSpec((FFN_TF, d), lambda i, j: (j, 0)),
            pl.BlockSpec((1, d), lambda i, j: (0, 0)),
        ],
        out_specs=pl.BlockSpec((FFN_TM, d), lambda i, j: (i, 0)),
        out_shape=jax.ShapeDtypeStruct((t, d), F32),
        scratch_shapes=[pltpu.VMEM((FFN_TM, d), BF16)],
        compiler_params=pltpu.CompilerParams(
            dimension_semantics=("parallel", "arbitrary"), vmem_limit_bytes=VMEM_LIMIT),
        name="ffn_final" if final_norm else "ffn",
    )(x, g, wg, wu, wd, fg)


INP_TM = 512
INP_SUB = 2


def _rope_dup(r, cos_t, sin_t):
    return r * cos_t + pltpu.roll(r, LANES - QK_ROPE // 2, axis=1) * sin_t


def _inproj_kernel(x_ref, g_ref, w_ref, qan_ref, wqup_ref, kvan_ref, wkvup_ref, qn_ref, kn_ref,
                   cos_ref, sin_ref, u_ref, q_ref, k_ref, v_ref, wbf_ref):
    sub = INP_TM // INP_SUB
    inv_head = 1.0 / QK_HEAD
    d_in = w_ref.shape[1]

    @pl.when(pl.program_id(0) == 0)
    def _():
        wbf_ref[:, :d_in] = w_ref[...].astype(BF16)
        wbf_ref[:, d_in:] = w_ref[:, d_in - QK_ROPE:].astype(BF16)

    for sb in range(INP_SUB):
        rows = slice(sb * sub, (sb + 1) * sub)
        h = _rms(x_ref[rows, :], g_ref[...]).astype(BF16)
        u_ref[rows, :] = jnp.dot(h, wbf_ref[:, :SSM_WIDTH], preferred_element_type=F32)
        lat = jnp.dot(h, wbf_ref[:, SSM_WIDTH:], preferred_element_type=F32)
        ql = lat[:, :Q_LORA]
        kvl = lat[:, Q_LORA:Q_LORA + KV_LORA]
        kpe = lat[:, Q_LORA + KV_LORA:]
        q = jnp.dot(_rms(ql, qan_ref[...]).astype(BF16), wqup_ref[...], preferred_element_type=F32)
        kv = jnp.dot(_rms(kvl, kvan_ref[...]).astype(BF16), wkvup_ref[...], preferred_element_type=F32)
        cos_t = cos_ref[rows, :]
        sin_t = sin_ref[rows, :]
        qn = qn_ref[...]
        kn = kn_ref[...]
        kpe_rot = _rope_dup(kpe * kn[:, LANES:], cos_t, sin_t)
        kpe_ssq = 0.5 * jnp.sum(kpe * kpe, axis=-1, keepdims=True)
        for hd in range(N_HEADS):
            qno = q[:, hd * HEAD_PAD:hd * HEAD_PAD + QK_NOPE]
            qro = q[:, hd * HEAD_PAD + QK_NOPE:(hd + 1) * HEAD_PAD]
            r = lax.rsqrt(jnp.sum(qno * qno + 0.5 * (qro * qro), axis=-1, keepdims=True) * inv_head + EPS)
            q_ref[hd, rows, :LANES] = (qno * r * qn[:, :LANES]).astype(BF16)
            q_ref[hd, rows, LANES:] = _rope_dup(qro * r * qn[:, LANES:], cos_t, sin_t).astype(BF16)
            kno = kv[:, hd * HEAD_PAD:hd * HEAD_PAD + QK_NOPE]
            r = lax.rsqrt((jnp.sum(kno * kno, axis=-1, keepdims=True) + kpe_ssq) * inv_head + EPS)
            k_ref[hd, rows, :LANES] = (kno * r * kn[:, :LANES]).astype(BF16)
            k_ref[hd, rows, LANES:] = (kpe_rot * r).astype(BF16)
            v_ref[hd, rows, :] = kv[:, hd * HEAD_PAD + QK_NOPE:(hd + 1) * HEAD_PAD].astype(BF16)


def _inproj(x, g, w, qan, wqup, kvan, wkvup, qn, kn, cos_t, sin_t):
    t, d = x.shape
    tm = INP_TM
    row = lambda i: (i, 0)
    head_row = lambda i: (0, i, 0)
    return pl.pallas_call(
        _inproj_kernel,
        grid=(t // tm,),
        in_specs=[
            pl.BlockSpec((tm, d), row),
            _const_spec(g.shape), _const_spec(w.shape), _const_spec(qan.shape), _const_spec(wqup.shape),
            _const_spec(kvan.shape), _const_spec(wkvup.shape), _const_spec(qn.shape), _const_spec(kn.shape),
            pl.BlockSpec((tm, LANES), row),
            pl.BlockSpec((tm, LANES), row),
        ],
        out_specs=[
            pl.BlockSpec((tm, SSM_WIDTH), row),
            pl.BlockSpec((N_HEADS, tm, HEAD_PAD), head_row),
            pl.BlockSpec((N_HEADS, tm, HEAD_PAD), head_row),
            pl.BlockSpec((N_HEADS, tm, V_HEAD), head_row),
        ],
        out_shape=[
            jax.ShapeDtypeStruct((t, SSM_WIDTH), F32),
            jax.ShapeDtypeStruct((N_HEADS, t, HEAD_PAD), BF16),
            jax.ShapeDtypeStruct((N_HEADS, t, HEAD_PAD), BF16),
            jax.ShapeDtypeStruct((N_HEADS, t, V_HEAD), BF16),
        ],
        scratch_shapes=[pltpu.VMEM((d, w.shape[1] + QK_ROPE), BF16)],
        compiler_params=pltpu.CompilerParams(
            dimension_semantics=("arbitrary",), vmem_limit_bytes=VMEM_LIMIT),
        name="in_proj",
    )(x, g, w, qan, wqup, kvan, wkvup, qn, kn, cos_t, sin_t)


S5_GF = SUBLANES
S5_RC = 128
S5_FR = S5_GF * S5_RC
S5_LT = SSM_WIDTH // LANES
S5_BLK = LANES // SSM_GROUP
S5_NP = S5_LT // 2
S5_UP = S5_RC + SUBLANES
S5_NV = S5_BLK * S5_NP * 2 // SUBLANES
S5_PITCH = (S5_NV + 1) * SUBLANES
S5_UNROLL = 8


def _frame_exchange(x, res, inverse):
    def skew(v):
        return pltpu.roll(v, 0, 1, stride=1, stride_axis=0)

    y = x if inverse else skew(x)
    for bit in (1, 2, 4):
        y = jnp.where((res & bit) != 0, pltpu.roll(y, SUBLANES - bit, 0), y)
    return skew(y) if inverse else y


def _exchange_maps():
    r = np.arange(S5_BLK)[:, None]
    q, c = np.divmod(np.arange(LANES)[None, :], S5_BLK)
    in_a = (r + c) % S5_GF
    in_ch = (q - (c < in_a)) % SSM_GROUP
    out_a = (r - c) % S5_GF
    out_ch = (q + (c + out_a >= S5_BLK)) % SSM_GROUP
    return in_a, in_ch, out_a, out_ch


def _s5g_kernel(u_ref, w1_ref, w2_ref, w3_ref, lr_ref, li_ref, d_ref, wglu_ref, bglu_ref, on_ref,
                y_ref, ex_ref, ey_ref, cs_ref, sp_ref, yn_ref, carry_ref):
    @pl.when(pl.program_id(1) == 0)
    def _():
        carry_ref[...] = jnp.zeros_like(carry_ref)

    res = lax.broadcasted_iota(jnp.int32, (SUBLANES, LANES), 1) % S5_BLK

    def fwd(k, carry):
        r0 = pl.multiple_of(k * S5_GF, S5_GF)
        for m in range(S5_LT):
            x = u_ref[pl.ds(r0, S5_GF), m * LANES:(m + 1) * LANES]
            ex_ref[m, pl.ds(k, S5_BLK, stride=S5_UP), :] = _frame_exchange(x, res, False)
        return carry

    lax.fori_loop(0, S5_RC, fwd, 0, unroll=S5_UNROLL)

    def lhs_u(bl, p):
        r = (-bl) % S5_BLK
        rows = slice(r * S5_UP, r * S5_UP + S5_RC)
        return jnp.concatenate([ex_ref[2 * p, rows, :], ex_ref[2 * p + 1, rows, :]], axis=1).astype(BF16)

    for bl in range(S5_BLK):
        for p in range(S5_NP):
            c = jnp.dot(lhs_u(bl, p), w1_ref[bl, p], preferred_element_type=F32)
            t = bl * S5_NP + p
            for part in range(2):
                t0 = (part * S5_BLK * S5_NP + t) * SUBLANES
                cs_ref[:, t0:t0 + SUBLANES, :] = (c[:, part * LANES:(part + 1) * LANES]
                                                  .reshape(S5_RC // SUBLANES, SUBLANES, LANES))

    lam = [(lr_ref[v], li_ref[v]) for v in range(S5_NV // 2)]
    nre = S5_NV // 2

    def step(kr, carry):
        for s in range(SUBLANES):
            row0 = pl.multiple_of(kr * (SUBLANES * S5_PITCH), SUBLANES) + s * S5_PITCH
            nxt_r, nxt_i = [], []
            for v in range(nre):
                sr, si = carry[v], carry[nre + v]
                sp_ref[pl.ds(row0 + v * SUBLANES, SUBLANES), :] = sr
                sp_ref[pl.ds(row0 + (nre + v) * SUBLANES, SUBLANES), :] = si
                cr = cs_ref[kr, pl.ds(v * SUBLANES * SUBLANES + s, SUBLANES, stride=SUBLANES), :]
                ci = cs_ref[kr, pl.ds((nre + v) * SUBLANES * SUBLANES + s, SUBLANES, stride=SUBLANES), :]
                lr, li = lam[v]
                nxt_r.append(lr * sr - li * si + cr)
                nxt_i.append(lr * si + li * sr + ci)
            carry = tuple(nxt_r + nxt_i)
        return carry

    init = tuple(carry_ref[v] for v in range(S5_NV))
    fin = lax.fori_loop(0, S5_RC // SUBLANES, step, init)
    for v in range(S5_NV):
        carry_ref[v] = fin[v]

    for bl in range(S5_BLK):
        for p in range(S5_NP):
            v, i = divmod(bl * S5_NP + p, SUBLANES)
            s_re = sp_ref[pl.ds(v * SUBLANES + i, S5_RC, stride=S5_PITCH), :]
            s_im = sp_ref[pl.ds((nre + v) * SUBLANES + i, S5_RC, stride=S5_PITCH), :]
            lhs_s = jnp.concatenate([s_re, s_im], axis=1).astype(BF16)
            y = (jnp.dot(lhs_s, w2_ref[bl, p], preferred_element_type=F32)
                 + jnp.dot(lhs_u(bl, p), w3_ref[bl, p], preferred_element_type=F32))
            rows = slice(bl * S5_UP, bl * S5_UP + S5_RC)
            ey_ref[2 * p, rows, :] = y[:, :LANES]
            ey_ref[2 * p + 1, rows, :] = y[:, LANES:]

    def inv(k, carry):
        r0 = pl.multiple_of(k * S5_GF, S5_GF)
        for m in range(S5_LT):
            x = ey_ref[m, pl.ds(k, S5_BLK, stride=S5_UP), :]
            yn_ref[pl.ds(r0, S5_GF), m * LANES:(m + 1) * LANES] = _frame_exchange(x, res, True)
        return carry

    lax.fori_loop(0, S5_RC, inv, 0, unroll=S5_UNROLL)

    for rb in range(S5_FR // MXU_DIM):
        rows = slice(rb * MXU_DIM, (rb + 1) * MXU_DIM)
        y = yn_ref[rows, :] + d_ref[...] * u_ref[rows, :]
        y = jax.nn.gelu(y)
        gate = jnp.dot(y.astype(BF16), wglu_ref[...], preferred_element_type=F32) + bglu_ref[...]
        y_ref[rows, :] = _rms(y * jax.nn.sigmoid(gate), on_ref[...]).astype(BF16)


def _s5g(u, w1, w2, w3, lr, li, d, wglu, bglu, on, batch, seq):
    nchunk = seq // S5_FR
    row = lambda b, c: (b * nchunk + c, 0)
    return pl.pallas_call(
        _s5g_kernel,
        grid=(batch, nchunk),
        in_specs=[
            pl.BlockSpec((S5_FR, SSM_WIDTH), row),
            _const_spec(w1.shape), _const_spec(w2.shape), _const_spec(w3.shape),
            _const_spec(lr.shape), _const_spec(li.shape),
            _const_spec(d.shape), _const_spec(wglu.shape), _const_spec(bglu.shape), _const_spec(on.shape),
        ],
        out_specs=pl.BlockSpec((S5_FR, SSM_WIDTH), row),
        out_shape=jax.ShapeDtypeStruct((batch * seq, SSM_WIDTH), BF16),
        scratch_shapes=[
            pltpu.VMEM((S5_LT, S5_BLK * S5_UP, LANES), F32),
            pltpu.VMEM((S5_LT, S5_BLK * S5_UP, LANES), F32),
            pltpu.VMEM((S5_RC // SUBLANES, S5_NV * SUBLANES * SUBLANES, LANES), F32),
            pltpu.VMEM((S5_RC * S5_PITCH, LANES), F32),
            pltpu.VMEM((S5_FR, SSM_WIDTH), F32),
            pltpu.VMEM((S5_NV, SUBLANES, LANES), F32),
        ],
        compiler_params=pltpu.CompilerParams(
            dimension_semantics=("parallel", "arbitrary"), vmem_limit_bytes=VMEM_LIMIT),
        name="s5",
    )(u, w1, w2, w3, lr, li, d, wglu, bglu, on)


def _s5g_params(log_step, a_re, a_im, b_re, b_im, c_re, c_im):
    G, N, C, GF = SSM_GROUPS, SSM_STATE, SSM_GROUP, S5_GF
    hi = lax.Precision.HIGHEST
    dt = jnp.exp(log_step)[:, None]
    mag = jnp.exp(a_re * dt)
    ang = a_im * dt
    lr, li = mag * jnp.cos(ang), mag * jnp.sin(ang)
    den = a_re * a_re + a_im * a_im
    fr = ((lr - 1.0) * a_re + li * a_im) / den
    fi = (li * a_re - (lr - 1.0) * a_im) / den
    bb_r = fr[..., None] * b_re - fi[..., None] * b_im
    bb_i = fr[..., None] * b_im + fi[..., None] * b_re
    pr, pi = [jnp.ones_like(lr)], [jnp.zeros_like(lr)]
    for _ in range(GF):
        pr, pi = pr + [pr[-1] * lr - pi[-1] * li], pi + [pr[-1] * li + pi[-1] * lr]
    pr, pi = jnp.stack(pr), jnp.stack(pi)
    e_r = pr[..., None] * bb_r - pi[..., None] * bb_i
    e_i = pr[..., None] * bb_i + pi[..., None] * bb_r
    q_r = c_re * pr[:, :, None, :] - c_im * pi[:, :, None, :]
    q_i = c_re * pi[:, :, None, :] + c_im * pr[:, :, None, :]
    k3 = (jnp.einsum('dgcn,gni->dgci', q_r, bb_r, precision=hi)
          - jnp.einsum('dgcn,gni->dgci', q_i, bb_i, precision=hi))

    def pairs(w):
        return w.reshape((S5_NP, 2, S5_BLK) + w.shape[1:]).transpose((2, 0, 1) + tuple(range(3, w.ndim + 2)))

    n1_r = pairs(e_r[GF - 1::-1].transpose(1, 0, 3, 2).reshape(G, LANES, N))
    n1_i = pairs(e_i[GF - 1::-1].transpose(1, 0, 3, 2).reshape(G, LANES, N))
    n2_r = pairs(q_r[1:].transpose(1, 3, 0, 2).reshape(G, N, LANES))
    n2_i = pairs(-q_i[1:].transpose(1, 3, 0, 2).reshape(G, N, LANES))
    frames = np.arange(GF)
    toep = (frames[None, :, None] - frames[:, None, None] == frames[None, None, :]).astype(np.float32)
    n3 = pairs(jnp.einsum('iod,dgxc->gicox', toep, k3[:GF], precision=hi).reshape(G, LANES, LANES))

    in_a, in_ch, out_a, out_ch = _exchange_maps()
    nat = np.arange(LANES)[None, None, :]
    bl = np.arange(S5_BLK)
    p_in = ((in_a * C + in_ch)[(-bl) % S5_BLK][:, :, None] == nat).astype(np.float32)
    p_out = ((out_a * C + out_ch)[bl][:, :, None] == nat).astype(np.float32)
    w1_r = jnp.einsum('bjx,bphxn->bphjn', p_in, n1_r, precision=hi)
    w1_i = jnp.einsum('bjx,bphxn->bphjn', p_in, n1_i, precision=hi)
    w2_r = jnp.einsum('bjx,bphnx->bphnj', p_out, n2_r, precision=hi)
    w2_i = jnp.einsum('bjx,bphnx->bphnj', p_out, n2_i, precision=hi)
    w3 = jnp.einsum('bjx,bphxy,bky->bphjk', p_in, n3, p_out, precision=hi)

    def tile(blocks):
        return jnp.concatenate([jnp.concatenate(r, axis=-1) for r in blocks], axis=-2).astype(BF16)

    a_r, a_i, b_r, b_i, w3p = w1_r, w1_i, w2_r, w2_i, w3
    z1, z2, z3 = jnp.zeros_like(a_r[:, :, 0]), jnp.zeros_like(b_r[:, :, 0]), jnp.zeros_like(w3p[:, :, 0])
    w1 = tile([[a_r[:, :, 0], z1, a_i[:, :, 0], z1], [z1, a_r[:, :, 1], z1, a_i[:, :, 1]]])
    w2 = tile([[b_r[:, :, 0], z2], [z2, b_r[:, :, 1]], [b_i[:, :, 0], z2], [z2, b_i[:, :, 1]]])
    w3t = tile([[w3p[:, :, 0], z3], [z3, w3p[:, :, 1]]])

    def state_vregs(x):
        return pairs(x).reshape(S5_NV // 2, SUBLANES, LANES)

    return w1, w2, w3t, state_vregs(pr[GF]), state_vregs(pi[GF])


ATT_TQ = 512


def _attn_kernel(q_ref, k_ref, v_ref, bias_ref, o_ref):
    seq = q_ref.shape[0]
    scale = QK_HEAD ** -0.5 * math.log2(math.e)
    nt = (((1,), (1,)), ((), ()))
    bias = bias_ref[...]
    for qi in range(seq // ATT_TQ):
        r0 = qi * ATT_TQ
        q = q_ref[r0:r0 + ATT_TQ, :]
        s_d = lax.dot_general(q, k_ref[r0:r0 + ATT_TQ, :], nt, preferred_element_type=F32) * scale + bias
        m = jnp.max(s_d, axis=-1, keepdims=True)
        if qi > 0:
            s_o = lax.dot_general(q, k_ref[:r0, :], nt, preferred_element_type=F32) * scale
            m = jnp.maximum(m, jnp.max(s_o, axis=-1, keepdims=True))
            p_o = jnp.exp2(s_o - m)
            l = jnp.sum(p_o, axis=-1, keepdims=True)
            acc = jnp.dot(p_o.astype(BF16), v_ref[:r0, :], preferred_element_type=F32)
        p_d = jnp.exp2(s_d - m)
        l_d = jnp.sum(p_d, axis=-1, keepdims=True)
        acc_d = jnp.dot(p_d.astype(BF16), v_ref[r0:r0 + ATT_TQ, :], preferred_element_type=F32)
        if qi > 0:
            l_d = l_d + l
            acc_d = acc_d + acc
        o_ref[r0:r0 + ATT_TQ, :] = acc_d / l_d


def _attention(q, k, v, batch, seq):
    qc = lax.broadcasted_iota(jnp.int32, (ATT_TQ, ATT_TQ), 0) // CHUNK
    kc = lax.broadcasted_iota(jnp.int32, (ATT_TQ, ATT_TQ), 1) // CHUNK
    bias = jnp.where(kc <= qc, 0.0, -jnp.inf).astype(F32)
    return pl.pallas_call(
        _attn_kernel,
        grid=(batch, N_HEADS),
        in_specs=[
            pl.BlockSpec((None, seq, HEAD_PAD), lambda b, h: (h, b, 0)),
            pl.BlockSpec((None, seq, HEAD_PAD), lambda b, h: (h, b, 0)),
            pl.BlockSpec((None, seq, V_HEAD), lambda b, h: (h, b, 0)),
            _const_spec(bias.shape),
        ],
        out_specs=pl.BlockSpec((seq, V_HEAD), lambda b, h: (b, h)),
        out_shape=jax.ShapeDtypeStruct((batch * seq, ATTN_WIDTH), F32),
        compiler_params=pltpu.CompilerParams(
            dimension_semantics=("parallel", "parallel"), vmem_limit_bytes=VMEM_LIMIT),
        name="attention",
    )(q, k, v, bias)


OUT_TM = 512


def _outproj_kernel(x_ref, ys_ref, ya_ref, an_ref, w_ref, o_ref, wbf_ref):
    @pl.when(pl.program_id(0) == 0)
    def _():
        wbf_ref[...] = w_ref[...].astype(BF16)

    ya = _rms(ya_ref[...], an_ref[...]).astype(BF16)
    o_ref[...] = (x_ref[...]
                  + jnp.dot(ys_ref[...], wbf_ref[:SSM_WIDTH, :], preferred_element_type=F32)
                  + jnp.dot(ya, wbf_ref[SSM_WIDTH:, :], preferred_element_type=F32))


def _outproj(x, ys, ya, an, w):
    t, d = x.shape
    tm = OUT_TM
    row = lambda i: (i, 0)
    return pl.pallas_call(
        _outproj_kernel,
        grid=(t // tm,),
        in_specs=[
            pl.BlockSpec((tm, d), row),
            pl.BlockSpec((tm, SSM_WIDTH), row),
            pl.BlockSpec((tm, ATTN_WIDTH), row),
            _const_spec(an.shape), _const_spec(w.shape),
        ],
        out_specs=pl.BlockSpec((tm, d), row),
        out_shape=jax.ShapeDtypeStruct((t, d), F32),
        scratch_shapes=[pltpu.VMEM(w.shape, BF16)],
        compiler_params=pltpu.CompilerParams(
            dimension_semantics=("arbitrary",), vmem_limit_bytes=VMEM_LIMIT),
        name="out_proj",
    )(x, ys, ya, an, w)


def _dup_rope(w):
    k = w.shape[0]
    w = w.reshape(k, -1, QK_HEAD)
    return jnp.concatenate([w, w[:, :, QK_NOPE:]], axis=2).reshape(k, -1)


def kernel(x, positions, ffn1_norm, ffn1_w_gate, ffn1_w_up, ffn1_w_down, mix_norm, w_in, ssm_log_step, ssm_a_re, ssm_a_im, ssm_b_re, ssm_b_im, ssm_c_re, ssm_c_im, ssm_d, ssm_w_glu, ssm_b_glu, mla_q_a_norm, mla_w_q_up, mla_kv_a_norm, mla_w_kv_up, mla_q_norm, mla_k_norm, ssm_out_norm, attn_out_norm, w_out, ffn2_norm, ffn2_w_gate, ffn2_w_up, ffn2_w_down, final_norm):
    batch, seq, d = x.shape
    depth = w_in.shape[0]
    t = batch * seq
    xt = x.reshape(t, d)

    inv_freq = ROPE_THETA ** (-jnp.arange(0, QK_ROPE, 2, dtype=F32) / QK_ROPE)
    half = QK_ROPE // 2
    lane = jnp.arange(LANES)
    ang = positions.astype(F32).reshape(t, 1) * jnp.tile(inv_freq, LANES // half)
    cos_t = jnp.where(lane < QK_ROPE, jnp.cos(ang), 0.0)
    sin_t = jnp.where(lane < half, -jnp.sin(ang), jnp.where(lane < QK_ROPE, jnp.sin(ang), 0.0))

    def ssm_order(w, axis):
        shp = w.shape
        w = w.reshape(shp[:axis] + (S5_LT, S5_BLK, SSM_GROUP) + shp[axis + 1:])
        return jnp.swapaxes(w, axis + 1, axis + 2).reshape(shp)

    row = lambda v: v.reshape(1, -1)
    for l in range(depth):
        xt = _ffn(xt, row(ffn1_norm[l]), ffn1_w_gate[l], ffn1_w_up[l], ffn1_w_down[l], row(final_norm[l]),
                  final_norm=False)

        qn = _dup_rope(row(mla_q_norm[l]))
        kn = _dup_rope(row(mla_k_norm[l]))
        w = jnp.concatenate([ssm_order(w_in[l][:, :SSM_WIDTH], 1), w_in[l][:, SSM_WIDTH:]], axis=1)
        u, q, k, v = _inproj(
            xt, row(mix_norm[l]), w, row(mla_q_a_norm[l]), _dup_rope(mla_w_q_up[l]).astype(BF16),
            row(mla_kv_a_norm[l]), mla_w_kv_up[l].astype(BF16), qn, kn, cos_t, sin_t)

        w1, w2, w3, lam_r, lam_i = _s5g_params(ssm_log_step[l], ssm_a_re[l], ssm_a_im[l], ssm_b_re[l],
                                               ssm_b_im[l], ssm_c_re[l], ssm_c_im[l])
        ys = _s5g(u, w1, w2, w3, lam_r, lam_i, ssm_order(row(ssm_d[l]), 1),
                  ssm_order(ssm_order(ssm_w_glu[l], 0), 1).astype(BF16), ssm_order(row(ssm_b_glu[l]), 1),
                  ssm_order(row(ssm_out_norm[l]), 1), batch, seq)

        ya = _attention(q, k, v, batch, seq)

        wo = jnp.concatenate([ssm_order(w_out[l][:SSM_WIDTH], 0), w_out[l][SSM_WIDTH:]], axis=0)
        xt = _outproj(xt, ys, ya, row(attn_out_norm[l]), wo)

        xt = _ffn(xt, row(ffn2_norm[l]), ffn2_w_gate[l], ffn2_w_up[l], ffn2_w_down[l], row(final_norm[l]),
                  final_norm=True)
    return xt.reshape(batch, seq, d)
```

```python
import functools
import math

import jax
import jax.numpy as jnp
import numpy as np
from jax import lax
from jax.experimental import pallas as pl
from jax.experimental.pallas import tpu as pltpu

D_MODEL = 2048
CHUNK = 64
SSM_WIDTH = 1024
SSM_GROUP = 16
SSM_GROUPS = 64
SSM_STATE = 64
N_HEADS = 8
QK_NOPE = 128
QK_ROPE = 64
QK_HEAD = QK_NOPE + QK_ROPE
V_HEAD = 128
ATTN_WIDTH = N_HEADS * V_HEAD
Q_LORA = 512
KV_LORA = 256
D_FF = 5632
ROPE_THETA = 10000.0
EPS = 1e-6

LANES = 128
SUBLANES = 8
MXU_DIM = 256
HEAD_PAD = 2 * LANES
VMEM_LIMIT = 60 * 1024 * 1024

BF16 = jnp.bfloat16
F32 = jnp.float32


def _rms(xf, g):
    return xf * lax.rsqrt(jnp.mean(xf * xf, axis=-1, keepdims=True) + EPS) * g


def _const_spec(shape):
    nd = len(shape)
    return pl.BlockSpec(shape, lambda *_: (0,) * nd, pipeline_mode=pl.Buffered(1))


FFN_TM = 1024
FFN_TF = 256


def _ffn_kernel(x_ref, g_ref, wg_ref, wu_ref, wd_ref, fg_ref, o_ref, h_ref, *, final_norm):
    j = pl.program_id(1)

    @pl.when(j == 0)
    def _():
        x = x_ref[...]
        h_ref[...] = _rms(x, g_ref[...]).astype(BF16)
        o_ref[...] = x

    h = h_ref[...]
    gate = jnp.dot(h, wg_ref[...].astype(BF16), preferred_element_type=F32)
    up = jnp.dot(h, wu_ref[...].astype(BF16), preferred_element_type=F32)
    act = (gate * jax.nn.sigmoid(gate) * (0.5 * up)).astype(BF16)
    o_ref[...] += jnp.dot(act, wd_ref[...].astype(BF16), preferred_element_type=F32)

    if final_norm:
        @pl.when(j == pl.num_programs(1) - 1)
        def _():
            o_ref[...] = _rms(o_ref[...], fg_ref[...])


def _ffn(x, g, wg, wu, wd, fg, final_norm):
    t, d = x.shape
    dff = wg.shape[1]
    grid = (t // FFN_TM, dff // FFN_TF)
    return pl.pallas_call(
        functools.partial(_ffn_kernel, final_norm=final_norm),
        grid=grid,
        in_specs=[
            pl.BlockSpec((FFN_TM, d), lambda i, j: (i, 0)),
            pl.BlockSpec((1, d), lambda i, j: (0, 0)),
            pl.BlockSpec((d, FFN_TF), lambda i, j: (0, j)),
            pl.BlockSpec((d, FFN_TF), lambda i, j: (0, j)),
            pl.BlockSpec((FFN_TF, d), lambda i, j: (j, 0)),
            pl.BlockSpec((1, d), lambda i, j: (0, 0)),
        ],
        out_specs=pl.BlockSpec((FFN_TM, d), lambda i, j: (i, 0)),
        out_shape=jax.ShapeDtypeStruct((t, d), F32),
        scratch_shapes=[pltpu.VMEM((FFN_TM, d), BF16)],
        compiler_params=pltpu.CompilerParams(
            dimension_semantics=("parallel", "arbitrary"), vmem_limit_bytes=VMEM_LIMIT),
        name="ffn_final" if final_norm else "ffn",
    )(x, g, wg, wu, wd, fg)


INP_TM = 512
INP_SUB = 2


def _rope_dup(r, cos_t, sin_t):
    return r * cos_t + pltpu.roll(r, LANES - QK_ROPE // 2, axis=1) * sin_t


def _inproj_kernel(x_ref, g_ref, w_ref, qan_ref, wqup_ref, kvan_ref, wkvup_ref, qn_ref, kn_ref,
                   cos_ref, sin_ref, perm_ref, u_ref, q_ref, k_ref, v_ref, wbf_ref):
    sub = INP_TM // INP_SUB
    inv_head = 1.0 / QK_HEAD
    d_in = w_ref.shape[1]

    @pl.when(pl.program_id(0) == 0)
    def _():
        for m in range(SSM_WIDTH // LANES):
            cols = slice(m * LANES, (m + 1) * LANES)
            wbf_ref[:, cols] = jnp.dot(w_ref[:, cols].astype(BF16), perm_ref[...],
                                       preferred_element_type=F32).astype(BF16)
        wbf_ref[:, SSM_WIDTH:d_in] = w_ref[:, SSM_WIDTH:].astype(BF16)
        wbf_ref[:, d_in:] = w_ref[:, d_in - QK_ROPE:].astype(BF16)

    for sb in range(INP_SUB):
        rows = slice(sb * sub, (sb + 1) * sub)
        h = _rms(x_ref[rows, :], g_ref[...]).astype(BF16)
        u_ref[rows, :] = jnp.dot(h, wbf_ref[:, :SSM_WIDTH], preferred_element_type=F32)
        lat = jnp.dot(h, wbf_ref[:, SSM_WIDTH:], preferred_element_type=F32)
        ql = lat[:, :Q_LORA]
        kvl = lat[:, Q_LORA:Q_LORA + KV_LORA]
        kpe = lat[:, Q_LORA + KV_LORA:]
        q = jnp.dot(_rms(ql, qan_ref[...]).astype(BF16), wqup_ref[...], preferred_element_type=F32)
        kv = jnp.dot(_rms(kvl, kvan_ref[...]).astype(BF16), wkvup_ref[...], preferred_element_type=F32)
        cos_t = cos_ref[rows, :]
        sin_t = sin_ref[rows, :]
        qn = qn_ref[...]
        kn = kn_ref[...]
        kpe_rot = _rope_dup(kpe * kn[:, LANES:], cos_t, sin_t)
        kpe_ssq = 0.5 * jnp.sum(kpe * kpe, axis=-1, keepdims=True)
        for hd in range(N_HEADS):
            qno = q[:, hd * HEAD_PAD:hd * HEAD_PAD + QK_NOPE]
            qro = q[:, hd * HEAD_PAD + QK_NOPE:(hd + 1) * HEAD_PAD]
            r = lax.rsqrt(jnp.sum(qno * qno + 0.5 * (qro * qro), axis=-1, keepdims=True) * inv_head + EPS)
            q_ref[hd, rows, :LANES] = (qno * r * qn[:, :LANES]).astype(BF16)
            q_ref[hd, rows, LANES:] = _rope_dup(qro * r * qn[:, LANES:], cos_t, sin_t).astype(BF16)
            kno = kv[:, hd * HEAD_PAD:hd * HEAD_PAD + QK_NOPE]
            r = lax.rsqrt((jnp.sum(kno * kno, axis=-1, keepdims=True) + kpe_ssq) * inv_head + EPS)
            k_ref[hd, rows, :LANES] = (kno * r * kn[:, :LANES]).astype(BF16)
            k_ref[hd, rows, LANES:] = (kpe_rot * r).astype(BF16)
            v_ref[hd, rows, :] = kv[:, hd * HEAD_PAD + QK_NOPE:(hd + 1) * HEAD_PAD].astype(BF16)


def _inproj(x, g, w, qan, wqup, kvan, wkvup, qn, kn, cos_t, sin_t, perm):
    t, d = x.shape
    tm = INP_TM
    row = lambda i: (i, 0)
    head_row = lambda i: (0, i, 0)
    return pl.pallas_call(
        _inproj_kernel,
        grid=(t // tm,),
        in_specs=[
            pl.BlockSpec((tm, d), row),
            _const_spec(g.shape), _const_spec(w.shape), _const_spec(qan.shape), _const_spec(wqup.shape),
            _const_spec(kvan.shape), _const_spec(wkvup.shape), _const_spec(qn.shape), _const_spec(kn.shape),
            pl.BlockSpec((tm, LANES), row),
            pl.BlockSpec((tm, LANES), row),
            _const_spec(perm.shape),
        ],
        out_specs=[
            pl.BlockSpec((tm, SSM_WIDTH), row),
            pl.BlockSpec((N_HEADS, tm, HEAD_PAD), head_row),
            pl.BlockSpec((N_HEADS, tm, HEAD_PAD), head_row),
            pl.BlockSpec((N_HEADS, tm, V_HEAD), head_row),
        ],
        out_shape=[
            jax.ShapeDtypeStruct((t, SSM_WIDTH), F32),
            jax.ShapeDtypeStruct((N_HEADS, t, HEAD_PAD), BF16),
            jax.ShapeDtypeStruct((N_HEADS, t, HEAD_PAD), BF16),
            jax.ShapeDtypeStruct((N_HEADS, t, V_HEAD), BF16),
        ],
        scratch_shapes=[pltpu.VMEM((d, w.shape[1] + QK_ROPE), BF16)],
        compiler_params=pltpu.CompilerParams(
            dimension_semantics=("arbitrary",), vmem_limit_bytes=VMEM_LIMIT),
        name="in_proj",
    )(x, g, w, qan, wqup, kvan, wkvup, qn, kn, cos_t, sin_t, perm)


S5_GF = SUBLANES
S5_RC = 128
S5_FR = S5_GF * S5_RC
S5_LT = SSM_WIDTH // LANES
S5_BLK = LANES // SSM_GROUP
S5_NP = S5_LT // 2
S5_UP = S5_RC + SUBLANES
S5_NV = S5_BLK * S5_NP * 2 // SUBLANES
S5_PITCH = (S5_NV + 1) * SUBLANES
S5_UNROLL = 8


def _frame_exchange(x, res, inverse):
    def skew(v):
        return pltpu.roll(v, 0, 1, stride=1, stride_axis=0)

    y = x if inverse else skew(x)
    for bit in (1, 2, 4):
        y = jnp.where((res & bit) != 0, pltpu.roll(y, SUBLANES - bit, 0), y)
    return skew(y) if inverse else y


def _exchange_maps():
    r = np.arange(S5_BLK)[:, None]
    q, c = np.divmod(np.arange(LANES)[None, :], S5_BLK)
    in_a = (r + c) % S5_GF
    in_ch = (q - (c < in_a)) % SSM_GROUP
    out_a = (r - c) % S5_GF
    out_ch = (q + (c + out_a >= S5_BLK)) % SSM_GROUP
    return in_a, in_ch, out_a, out_ch


def _s5g_kernel(u_ref, w1_ref, w2_ref, w3_ref, lr_ref, li_ref, vec_ref, wglu_ref, perm_ref, permt_ref,
                y_ref, ex_ref, ey_ref, cs_ref, sp_ref, yn_ref, carry_ref, wg_ref):
    @pl.when(pl.program_id(1) == 0)
    def _():
        carry_ref[...] = jnp.zeros_like(carry_ref)

    @pl.when((pl.program_id(0) == 0) & (pl.program_id(1) == 0))
    def _():
        for m in range(S5_LT):
            rows = slice(m * LANES, (m + 1) * LANES)
            wg_ref[rows, :] = jnp.dot(permt_ref[...], wglu_ref[rows, :], preferred_element_type=F32).astype(BF16)
        for m in range(S5_LT):
            cols = slice(m * LANES, (m + 1) * LANES)
            wg_ref[:, cols] = jnp.dot(wg_ref[:, cols], perm_ref[...], preferred_element_type=F32).astype(BF16)

    d_skip, b_glu, out_gain = vec_ref[0:1, :], vec_ref[1:2, :], vec_ref[2:3, :]

    res = lax.broadcasted_iota(jnp.int32, (SUBLANES, LANES), 1) % S5_BLK

    def fwd(k, carry):
        r0 = pl.multiple_of(k * S5_GF, S5_GF)
        for m in range(S5_LT):
            x = u_ref[pl.ds(r0, S5_GF), m * LANES:(m + 1) * LANES]
            ex_ref[m, pl.ds(k, S5_BLK, stride=S5_UP), :] = _frame_exchange(x, res, False)
        return carry

    lax.fori_loop(0, S5_RC, fwd, 0, unroll=S5_UNROLL)

    def lhs_u(bl, p):
        r = (-bl) % S5_BLK
        rows = slice(r * S5_UP, r * S5_UP + S5_RC)
        return jnp.concatenate([ex_ref[2 * p, rows, :], ex_ref[2 * p + 1, rows, :]], axis=1).astype(BF16)

    for bl in range(S5_BLK):
        for p in range(S5_NP):
            c = jnp.dot(lhs_u(bl, p), w1_ref[bl, p], preferred_element_type=F32)
            t = bl * S5_NP + p
            for part in range(2):
                t0 = (part * S5_BLK * S5_NP + t) * SUBLANES
                cs_ref[:, t0:t0 + SUBLANES, :] = (c[:, part * LANES:(part + 1) * LANES]
                                                  .reshape(S5_RC // SUBLANES, SUBLANES, LANES))

    lam = [(lr_ref[v], li_ref[v]) for v in range(S5_NV // 2)]
    nre = S5_NV // 2

    def step(kr, carry):
        for s in range(SUBLANES):
            row0 = pl.multiple_of(kr * (SUBLANES * S5_PITCH), SUBLANES) + s * S5_PITCH
            nxt_r, nxt_i = [], []
            for v in range(nre):
                sr, si = carry[v], carry[nre + v]
                sp_ref[pl.ds(row0 + v * SUBLANES, SUBLANES), :] = sr
                sp_ref[pl.ds(row0 + (nre + v) * SUBLANES, SUBLANES), :] = si
                cr = cs_ref[kr, pl.ds(v * SUBLANES * SUBLANES + s, SUBLANES, stride=SUBLANES), :]
                ci = cs_ref[kr, pl.ds((nre + v) * SUBLANES * SUBLANES + s, SUBLANES, stride=SUBLANES), :]
                lr, li = lam[v]
                nxt_r.append(lr * sr - li * si + cr)
                nxt_i.append(lr * si + li * sr + ci)
            carry = tuple(nxt_r + nxt_i)
        return carry

    init = tuple(carry_ref[v] for v in range(S5_NV))
    fin = lax.fori_loop(0, S5_RC // SUBLANES, step, init)
    for v in range(S5_NV):
        carry_ref[v] = fin[v]

    for bl in range(S5_BLK):
        for p in range(S5_NP):
            v, i = divmod(bl * S5_NP + p, SUBLANES)
            s_re = sp_ref[pl.ds(v * SUBLANES + i, S5_RC, stride=S5_PITCH), :]
            s_im = sp_ref[pl.ds((nre + v) * SUBLANES + i, S5_RC, stride=S5_PITCH), :]
            lhs_s = jnp.concatenate([s_re, s_im], axis=1).astype(BF16)
            y = (jnp.dot(lhs_s, w2_ref[bl, p], preferred_element_type=F32)
                 + jnp.dot(lhs_u(bl, p), w3_ref[bl, p], preferred_element_type=F32))
            rows = slice(bl * S5_UP, bl * S5_UP + S5_RC)
            ey_ref[2 * p, rows, :] = y[:, :LANES]
            ey_ref[2 * p + 1, rows, :] = y[:, LANES:]

    def inv(k, carry):
        r0 = pl.multiple_of(k * S5_GF, S5_GF)
        for m in range(S5_LT):
            x = ey_ref[m, pl.ds(k, S5_BLK, stride=S5_UP), :]
            yn_ref[pl.ds(r0, S5_GF), m * LANES:(m + 1) * LANES] = _frame_exchange(x, res, True)
        return carry

    lax.fori_loop(0, S5_RC, inv, 0, unroll=S5_UNROLL)

    for rb in range(S5_FR // MXU_DIM):
        rows = slice(rb * MXU_DIM, (rb + 1) * MXU_DIM)
        y = yn_ref[rows, :] + d_skip * u_ref[rows, :]
        y = jax.nn.gelu(y)
        gate = jnp.dot(y.astype(BF16), wg_ref[...], preferred_element_type=F32) + b_glu
        y_ref[rows, :] = _rms(y * jax.nn.sigmoid(gate), out_gain).astype(BF16)


def _s5g(u, w1, w2, w3, lr, li, vec, wglu, perm, permt, batch, seq):
    nchunk = seq // S5_FR
    row = lambda b, c: (b * nchunk + c, 0)
    return pl.pallas_call(
        _s5g_kernel,
        grid=(batch, nchunk),
        in_specs=[
            pl.BlockSpec((S5_FR, SSM_WIDTH), row),
            _const_spec(w1.shape), _const_spec(w2.shape), _const_spec(w3.shape),
            _const_spec(lr.shape), _const_spec(li.shape),
            _const_spec(vec.shape), _const_spec(wglu.shape), _const_spec(perm.shape), _const_spec(permt.shape),
        ],
        out_specs=pl.BlockSpec((S5_FR, SSM_WIDTH), row),
        out_shape=jax.ShapeDtypeStruct((batch * seq, SSM_WIDTH), BF16),
        scratch_shapes=[
            pltpu.VMEM((S5_LT, S5_BLK * S5_UP, LANES), F32),
            pltpu.VMEM((S5_LT, S5_BLK * S5_UP, LANES), F32),
            pltpu.VMEM((S5_RC // SUBLANES, S5_NV * SUBLANES * SUBLANES, LANES), F32),
            pltpu.VMEM((S5_RC * S5_PITCH, LANES), F32),
            pltpu.VMEM((S5_FR, SSM_WIDTH), F32),
            pltpu.VMEM((S5_NV, SUBLANES, LANES), F32),
            pltpu.VMEM(wglu.shape, BF16),
        ],
        compiler_params=pltpu.CompilerParams(
            dimension_semantics=("arbitrary", "arbitrary"), vmem_limit_bytes=VMEM_LIMIT),
        name="s5",
    )(u, w1, w2, w3, lr, li, vec, wglu, perm, permt)


def _s5_tiles_kernel(n1_ref, n2_ref, n3_ref, pin_ref, pout_ref, w1_ref, w2_ref, w3_ref):
    pin = pin_ref[...]
    pout = pout_ref[...]
    for p in range(S5_NP):
        w1_ref[p] = jnp.dot(pin, n1_ref[p], preferred_element_type=F32).astype(BF16)
        w2_ref[p] = jnp.dot(n2_ref[p], pout, preferred_element_type=F32).astype(BF16)
        t = jnp.dot(pin, n3_ref[p], preferred_element_type=F32).astype(BF16)
        w3_ref[p] = jnp.dot(t, pout, preferred_element_type=F32).astype(BF16)


def _s5_tiles(n1, n2, n3, pin, pout):
    tiles = pl.BlockSpec((None,) + n1.shape[1:], lambda b: (b, 0, 0, 0))
    maps = pl.BlockSpec((None,) + pin.shape[1:], lambda b: (b, 0, 0))
    return pl.pallas_call(
        _s5_tiles_kernel,
        grid=(S5_BLK,),
        in_specs=[tiles, tiles, tiles, maps, maps],
        out_specs=[tiles, tiles, tiles],
        out_shape=[jax.ShapeDtypeStruct(n1.shape, BF16)] * 3,
        compiler_params=pltpu.CompilerParams(
            dimension_semantics=("parallel",), vmem_limit_bytes=VMEM_LIMIT),
        name="s5_tiles",
    )(n1, n2, n3, pin, pout)


def _s5g_params(log_step, a_re, a_im, b_re, b_im, c_re, c_im):
    G, N, C, GF = SSM_GROUPS, SSM_STATE, SSM_GROUP, S5_GF
    dt = jnp.exp(log_step)[:, None]
    mag = jnp.exp(a_re * dt)
    ang = a_im * dt
    lr, li = mag * jnp.cos(ang), mag * jnp.sin(ang)
    den = a_re * a_re + a_im * a_im
    fr = ((lr - 1.0) * a_re + li * a_im) / den
    fi = (li * a_re - (lr - 1.0) * a_im) / den
    pr, pi = [jnp.ones_like(lr)], [jnp.zeros_like(lr)]
    for _ in range(GF):
        pr, pi = pr + [pr[-1] * lr - pi[-1] * li], pi + [pr[-1] * li + pi[-1] * lr]
    pr, pi = jnp.stack(pr, axis=1), jnp.stack(pi, axis=1)
    mod2 = pr * pr + pi * pi
    ir, ii = pr / mod2, -pi / mod2
    bt_r, bt_i = b_re.transpose(0, 2, 1), b_im.transpose(0, 2, 1)
    bb_r = fr[:, None, :] * bt_r - fi[:, None, :] * bt_i
    bb_i = fr[:, None, :] * bt_i + fi[:, None, :] * bt_r

    def outer(ar, ai, br, bi):
        ar, ai, br, bi = ar[:, :, None, :], ai[:, :, None, :], br[:, None, :, :], bi[:, None, :, :]
        return (ar * br - ai * bi).reshape(G, LANES, N), (ar * bi + ai * br).reshape(G, LANES, N)

    n1_r, n1_i = outer(pr[:, GF - 1::-1], pi[:, GF - 1::-1], bb_r, bb_i)
    m1_r, m1_i = outer(ir[:, 1:], ii[:, 1:], bb_r, bb_i)
    q_r, q_i = outer(pr[:, 1:], pi[:, 1:], c_re, c_im)
    k3 = jnp.einsum('gyk,gxk->gyx', jnp.concatenate([m1_r, m1_i], axis=-1),
                    jnp.concatenate([q_r, -q_i], axis=-1), precision=lax.Precision.HIGHEST)
    fr_of = np.arange(LANES) // C
    n3 = jnp.where(fr_of[None, :, None] <= fr_of[None, None, :], k3, 0.0)

    def pairs(w):
        return w.reshape((S5_NP, 2, S5_BLK) + w.shape[1:]).transpose((2, 0, 1) + tuple(range(3, w.ndim + 2)))

    def tile(blocks):
        return jnp.concatenate([jnp.concatenate(r, axis=-1) for r in blocks], axis=-2).astype(BF16)

    a_r, a_i, w3p = pairs(n1_r), pairs(n1_i), pairs(n3)
    b_r, b_i = pairs(q_r.transpose(0, 2, 1)), pairs(-q_i.transpose(0, 2, 1))
    z1, z2, z3 = jnp.zeros_like(a_r[:, :, 0]), jnp.zeros_like(b_r[:, :, 0]), jnp.zeros_like(w3p[:, :, 0])
    n1 = tile([[a_r[:, :, 0], z1, a_i[:, :, 0], z1], [z1, a_r[:, :, 1], z1, a_i[:, :, 1]]])
    n2 = tile([[b_r[:, :, 0], z2], [z2, b_r[:, :, 1]], [b_i[:, :, 0], z2], [z2, b_i[:, :, 1]]])
    n3t = tile([[w3p[:, :, 0], z3], [z3, w3p[:, :, 1]]])

    in_a, in_ch, out_a, out_ch = _exchange_maps()
    nat = np.arange(LANES)[None, None, :]
    bl = np.arange(S5_BLK)
    p_in = (in_a * C + in_ch)[(-bl) % S5_BLK][:, :, None] == nat
    p_out = ((out_a * C + out_ch)[bl][:, :, None] == nat).transpose(0, 2, 1)
    two = np.eye(2)[None, :, None, :, None]
    pin2 = (p_in[:, None, :, None, :] * two).reshape(S5_BLK, 2 * LANES, 2 * LANES)
    pout2 = (p_out[:, None, :, None, :] * two).reshape(S5_BLK, 2 * LANES, 2 * LANES)
    w1, w2, w3 = _s5_tiles(n1, n2, n3t, jnp.asarray(pin2, BF16), jnp.asarray(pout2, BF16))

    def state_vregs(v):
        return pairs(v).reshape(S5_NV // 2, SUBLANES, LANES)

    return w1, w2, w3, state_vregs(pr[:, GF]), state_vregs(pi[:, GF])


ATT_TQ = 512


def _attn_kernel(q_ref, k_ref, v_ref, bias_ref, o_ref):
    seq = q_ref.shape[0]
    scale = QK_HEAD ** -0.5 * math.log2(math.e)
    nt = (((1,), (1,)), ((), ()))
    bias = bias_ref[...]
    for qi in range(seq // ATT_TQ):
        r0 = qi * ATT_TQ
        q = q_ref[r0:r0 + ATT_TQ, :]
        s_d = lax.dot_general(q, k_ref[r0:r0 + ATT_TQ, :], nt, preferred_element_type=F32) * scale + bias
        m = jnp.max(s_d, axis=-1, keepdims=True)
        if qi > 0:
            s_o = lax.dot_general(q, k_ref[:r0, :], nt, preferred_element_type=F32) * scale
            m = jnp.maximum(m, jnp.max(s_o, axis=-1, keepdims=True))
            p_o = jnp.exp2(s_o - m)
            l = jnp.sum(p_o, axis=-1, keepdims=True)
            acc = jnp.dot(p_o.astype(BF16), v_ref[:r0, :], preferred_element_type=F32)
        p_d = jnp.exp2(s_d - m)
        l_d = jnp.sum(p_d, axis=-1, keepdims=True)
        acc_d = jnp.dot(p_d.astype(BF16), v_ref[r0:r0 + ATT_TQ, :], preferred_element_type=F32)
        if qi > 0:
            l_d = l_d + l
            acc_d = acc_d + acc
        o_ref[r0:r0 + ATT_TQ, :] = acc_d / l_d


def _attention(q, k, v, batch, seq):
    qc = lax.broadcasted_iota(jnp.int32, (ATT_TQ, ATT_TQ), 0) // CHUNK
    kc = lax.broadcasted_iota(jnp.int32, (ATT_TQ, ATT_TQ), 1) // CHUNK
    bias = jnp.where(kc <= qc, 0.0, -jnp.inf).astype(F32)
    return pl.pallas_call(
        _attn_kernel,
        grid=(batch, N_HEADS),
        in_specs=[
            pl.BlockSpec((None, seq, HEAD_PAD), lambda b, h: (h, b, 0)),
            pl.BlockSpec((None, seq, HEAD_PAD), lambda b, h: (h, b, 0)),
            pl.BlockSpec((None, seq, V_HEAD), lambda b, h: (h, b, 0)),
            _const_spec(bias.shape),
        ],
        out_specs=pl.BlockSpec((seq, V_HEAD), lambda b, h: (b, h)),
        out_shape=jax.ShapeDtypeStruct((batch * seq, ATTN_WIDTH), F32),
        compiler_params=pltpu.CompilerParams(
            dimension_semantics=("parallel", "parallel"), vmem_limit_bytes=VMEM_LIMIT),
        name="attention",
    )(q, k, v, bias)


OUT_TM = 512


def _outproj_kernel(x_ref, ys_ref, ya_ref, an_ref, w_ref, permt_ref, o_ref, wbf_ref):
    @pl.when(pl.program_id(0) == 0)
    def _():
        for m in range(SSM_WIDTH // LANES):
            rows = slice(m * LANES, (m + 1) * LANES)
            wbf_ref[rows, :] = jnp.dot(permt_ref[...], w_ref[rows, :].astype(BF16),
                                       preferred_element_type=F32).astype(BF16)
        wbf_ref[SSM_WIDTH:, :] = w_ref[SSM_WIDTH:, :].astype(BF16)

    ya = _rms(ya_ref[...], an_ref[...]).astype(BF16)
    o_ref[...] = (x_ref[...]
                  + jnp.dot(ys_ref[...], wbf_ref[:SSM_WIDTH, :], preferred_element_type=F32)
                  + jnp.dot(ya, wbf_ref[SSM_WIDTH:, :], preferred_element_type=F32))


def _outproj(x, ys, ya, an, w, permt):
    t, d = x.shape
    tm = OUT_TM
    row = lambda i: (i, 0)
    return pl.pallas_call(
        _outproj_kernel,
        grid=(t // tm,),
        in_specs=[
            pl.BlockSpec((tm, d), row),
            pl.BlockSpec((tm, SSM_WIDTH), row),
            pl.BlockSpec((tm, ATTN_WIDTH), row),
            _const_spec(an.shape), _const_spec(w.shape), _const_spec(permt.shape),
        ],
        out_specs=pl.BlockSpec((tm, d), row),
        out_shape=jax.ShapeDtypeStruct((t, d), F32),
        scratch_shapes=[pltpu.VMEM(w.shape, BF16)],
        compiler_params=pltpu.CompilerParams(
            dimension_semantics=("arbitrary",), vmem_limit_bytes=VMEM_LIMIT),
        name="out_proj",
    )(x, ys, ya, an, w, permt)


def _dup_rope(w):
    k = w.shape[0]
    w = w.reshape(k, -1, QK_HEAD)
    return jnp.concatenate([w, w[:, :, QK_NOPE:]], axis=2).reshape(k, -1)


def kernel(x, positions, ffn1_norm, ffn1_w_gate, ffn1_w_up, ffn1_w_down, mix_norm, w_in, ssm_log_step, ssm_a_re, ssm_a_im, ssm_b_re, ssm_b_im, ssm_c_re, ssm_c_im, ssm_d, ssm_w_glu, ssm_b_glu, mla_q_a_norm, mla_w_q_up, mla_kv_a_norm, mla_w_kv_up, mla_q_norm, mla_k_norm, ssm_out_norm, attn_out_norm, w_out, ffn2_norm, ffn2_w_gate, ffn2_w_up, ffn2_w_down, final_norm):
    batch, seq, d = x.shape
    depth = w_in.shape[0]
    t = batch * seq
    xt = x.reshape(t, d)

    inv_freq = ROPE_THETA ** (-jnp.arange(0, QK_ROPE, 2, dtype=F32) / QK_ROPE)
    half = QK_ROPE // 2
    lane = jnp.arange(LANES)
    ang = positions.astype(F32).reshape(t, 1) * jnp.tile(inv_freq, LANES // half)
    cos_t = jnp.where(lane < QK_ROPE, jnp.cos(ang), 0.0)
    sin_t = jnp.where(lane < half, -jnp.sin(ang), jnp.where(lane < QK_ROPE, jnp.sin(ang), 0.0))

    lanes = np.arange(LANES)
    perm_np = lanes[:, None] == (lanes[None, :] % S5_BLK * SSM_GROUP + lanes[None, :] // S5_BLK)
    perm, permt = jnp.asarray(perm_np, BF16), jnp.asarray(perm_np.T, BF16)

    def ssm_order(w):
        w = w.reshape(w.shape[0], S5_LT, S5_BLK, SSM_GROUP)
        return jnp.swapaxes(w, 2, 3).reshape(w.shape[0], SSM_WIDTH)

    row = lambda v: v.reshape(1, -1)
    for l in range(depth):
        xt = _ffn(xt, row(ffn1_norm[l]), ffn1_w_gate[l], ffn1_w_up[l], ffn1_w_down[l], row(final_norm[l]),
                  final_norm=False)

        qn = _dup_rope(row(mla_q_norm[l]))
        kn = _dup_rope(row(mla_k_norm[l]))
        u, q, k, v = _inproj(
            xt, row(mix_norm[l]), w_in[l], row(mla_q_a_norm[l]), _dup_rope(mla_w_q_up[l]).astype(BF16),
            row(mla_kv_a_norm[l]), mla_w_kv_up[l].astype(BF16), qn, kn, cos_t, sin_t, perm)

        w1, w2, w3, lam_r, lam_i = _s5g_params(ssm_log_step[l], ssm_a_re[l], ssm_a_im[l], ssm_b_re[l],
                                               ssm_b_im[l], ssm_c_re[l], ssm_c_im[l])
        vec = ssm_order(jnp.stack([ssm_d[l].reshape(-1), ssm_b_glu[l], ssm_out_norm[l]]))
        ys = _s5g(u, w1, w2, w3, lam_r, lam_i, vec, ssm_w_glu[l].astype(BF16), perm, permt, batch, seq)

        ya = _attention(q, k, v, batch, seq)

        xt = _outproj(xt, ys, ya, row(attn_out_norm[l]), w_out[l], permt)

        xt = _ffn(xt, row(ffn2_norm[l]), ffn2_w_gate[l], ffn2_w_up[l], ffn2_w_down[l], row(final_norm[l]),
                  final_norm=True)
    return xt.reshape(batch, seq, d)
```
